```python
import jax, jax.numpy as jnp
from jax import lax
import numpy as np

D_MODEL = 1024
BATCH = 2
SEQ = 8192
DEPTH = 1

N_HEADS = 16
HEAD_DIM = 64
N_KV = 4
GROUP = N_HEADS // N_KV
ROPE_DIM = HEAD_DIM // 4
ROPE_THETA = 500000.0
CMP_BLOCK = 32
CMP_STRIDE = 16
CMP_HIDDEN = 4 * HEAD_DIM
SEL_BLOCK = 64
SEL_TOPK = 16
WINDOW = 512
Q_BLOCK = 128
CONV_DIM = D_MODEL
CONV_WIDTH = 31
D_FF = ((8 * D_MODEL // 3 + 255) // 256) * 256
EPS = 1e-6

kernel_name = "hybrid_nsa_conformer_gated_block"

SPLIT_SIZES = [N_HEADS * HEAD_DIM] + [N_KV * HEAD_DIM] * 6 + [3 * N_HEADS, CONV_DIM, CONV_DIM, D_MODEL, D_MODEL]
IN_COLS = sum(SPLIT_SIZES)


def rms_norm(x, g):
    xf = x.astype(jnp.float32)
    y = xf * lax.rsqrt(jnp.mean(xf * xf, axis=-1, keepdims=True) + EPS)
    return (y * g.astype(jnp.float32)).astype(x.dtype)


def rope_tables(positions, dtype):
    inv = ROPE_THETA ** (-jnp.arange(0, ROPE_DIM, 2, dtype=jnp.float32) / ROPE_DIM)
    ang = positions.astype(jnp.float32)[..., None] * inv
    return jnp.cos(ang).astype(dtype), jnp.sin(ang).astype(dtype)


def apply_partial_rope(t, cos, sin):
    half = ROPE_DIM // 2
    c = cos[:, :, None, :]
    s = sin[:, :, None, :]
    t1 = t[..., :half]
    t2 = t[..., half:ROPE_DIM]
    return jnp.concatenate([t1 * c - t2 * s, t2 * c + t1 * s, t[..., ROPE_DIM:]], axis=-1)


def masked_softmax(s, mask):
    s = jnp.where(mask, s, -jnp.inf)
    m = jnp.max(s, axis=-1, keepdims=True)
    m = jnp.where(jnp.isfinite(m), m, 0.0)
    e = jnp.where(mask, jnp.exp(s - m), 0.0)
    d = jnp.sum(e, axis=-1, keepdims=True)
    return e / jnp.where(d > 0, d, 1.0)


def compress_blocks(t, pos_emb, w1, w2):
    B, G, S, Dh = t.shape
    chunks = t.reshape(B, G, S // CMP_STRIDE, CMP_STRIDE, Dh)
    blocks = jnp.concatenate([chunks[:, :, :-1], chunks[:, :, 1:]], axis=3)
    blocks = (blocks + pos_emb).reshape(B, G, -1, CMP_BLOCK * Dh)
    return jax.nn.silu(blocks @ w1) @ w2


def gather_sel_blocks(t, idx):
    B, G, S, Dh = t.shape
    Qn, K = idx.shape[2], idx.shape[3]
    tb = t.reshape(B, G, S // SEL_BLOCK, SEL_BLOCK * Dh)
    g = jax.vmap(jax.vmap(lambda a, i: a[i]))(tb, idx.reshape(B, G, Qn * K))
    return g.reshape(B, G, Qn, K * SEL_BLOCK, Dh)


def nsa_attention(q, k_c, v_c, k_s, v_s, k_w, v_w, gates):
    B, G, R, S, Dh = q.shape
    dt = q.dtype
    scale = HEAD_DIM ** -0.5
    n_cmp = k_c.shape[2]
    n_sel = S // SEL_BLOCK
    top_k = min(SEL_TOPK, n_sel)
    c_end = jnp.arange(n_cmp) * CMP_STRIDE + CMP_BLOCK - 1
    ks = jnp.arange(n_cmp)[:, None] * CMP_STRIDE
    bs = jnp.arange(n_sel)[None, :] * SEL_BLOCK
    overlap = (jnp.clip(jnp.minimum(ks + CMP_BLOCK, bs + SEL_BLOCK) - jnp.maximum(ks, bs), 0)
               .astype(jnp.float32) / CMP_STRIDE)
    k_w_pad = jnp.pad(k_w, ((0, 0), (0, 0), (WINDOW, 0), (0, 0)))
    v_w_pad = jnp.pad(v_w, ((0, 0), (0, 0), (WINDOW, 0), (0, 0)))
    sel_off = jnp.arange(SEL_BLOCK)
    j_all = jnp.arange(n_sel)

    def one_block(i):
        t0 = i * Q_BLOCK
        qb = lax.dynamic_slice_in_dim(q, t0, Q_BLOCK, axis=3)
        gb = lax.dynamic_slice_in_dim(gates, t0, Q_BLOCK, axis=3)
        tpos = t0 + jnp.arange(Q_BLOCK)
        s_c = jnp.einsum('bgrqd,bgnd->bgrqn', qb, k_c).astype(jnp.float32) * scale
        p_c = masked_softmax(s_c, c_end[None, :] <= tpos[:, None])
        o_c = jnp.einsum('bgrqn,bgnd->bgrqd', p_c.astype(dt), v_c)
        imp = jnp.einsum('bgrqn,nj->bgqj', p_c, overlap)
        qblk = tpos // SEL_BLOCK
        forced = (j_all[None, :] == 0) | (j_all[None, :] == qblk[:, None]) | (j_all[None, :] == qblk[:, None] - 1)
        causal = j_all[None, :] <= qblk[:, None]
        score = jnp.where(forced, jnp.inf, jnp.where(causal, imp, -jnp.inf))
        vals, idx = lax.top_k(score, top_k)
        ks_g = gather_sel_blocks(k_s, idx)
        vs_g = gather_sel_blocks(v_s, idx)
        tok = idx[..., None] * SEL_BLOCK + sel_off
        m_s = ((vals != -jnp.inf)[..., None] & (tok <= tpos[:, None, None])).reshape(B, G, Q_BLOCK, top_k * SEL_BLOCK)
        s_s = jnp.einsum('bgrqd,bgqkd->bgrqk', qb, ks_g).astype(jnp.float32) * scale
        p_s = masked_softmax(s_s, m_s[:, :, None])
        o_s = jnp.einsum('bgrqk,bgqkd->bgrqd', p_s.astype(dt), vs_g)
        kw = lax.dynamic_slice_in_dim(k_w_pad, t0, WINDOW + Q_BLOCK, axis=2)
        vw = lax.dynamic_slice_in_dim(v_w_pad, t0, WINDOW + Q_BLOCK, axis=2)
        kpos = t0 - WINDOW + jnp.arange(WINDOW + Q_BLOCK)
        m_w = (kpos[None, :] <= tpos[:, None]) & (kpos[None, :] > tpos[:, None] - WINDOW) & (kpos[None, :] >= 0)
        s_w = jnp.einsum('bgrqd,bgkd->bgrqk', qb, kw).astype(jnp.float32) * scale
        p_w = masked_softmax(s_w, m_w)
        o_w = jnp.einsum('bgrqk,bgkd->bgrqd', p_w.astype(dt), vw)
        return gb[..., 0:1] * o_c + gb[..., 1:2] * o_s + gb[..., 2:3] * o_w

    out = lax.map(one_block, jnp.arange(S // Q_BLOCK))
    return out.transpose(1, 0, 4, 2, 3, 5).reshape(B, S, G * R * Dh)


def conformer_conv(u_a, u_b, conv_w, conv_b, ln_g, ln_b, w_o, b_o):
    z = u_a * jax.nn.sigmoid(u_b)
    zp = jnp.pad(z, ((0, 0), (CONV_WIDTH - 1, 0), (0, 0)))
    z = lax.conv_general_dilated(zp, conv_w, window_strides=(1,), padding='VALID',
                                 dimension_numbers=('NWC', 'WIO', 'NWC'),
                                 feature_group_count=CONV_DIM) + conv_b
    zf = z.astype(jnp.float32)
    mu = jnp.mean(zf, axis=-1, keepdims=True)
    var = jnp.mean(jnp.square(zf - mu), axis=-1, keepdims=True)
    zn = ((zf - mu) * lax.rsqrt(var + EPS) * ln_g.astype(jnp.float32) + ln_b.astype(jnp.float32)).astype(z.dtype)
    return jax.nn.silu(zn) @ w_o + b_o


def setup_inputs(seed: int = 0) -> dict:
    key = jax.random.key(seed)
    ks = jax.random.split(key, 24)
    f32 = jnp.float32
    nrm = lambda k, shape, s: jax.random.normal(k, shape, f32) * s
    d = D_MODEL
    return {
        'x': jax.random.normal(ks[0], (BATCH, SEQ, d), f32),
        'positions': jnp.broadcast_to(jnp.arange(SEQ, dtype=jnp.int32), (BATCH, SEQ)),
        'norm_mix_g': 1.0 + nrm(ks[1], (d,), 0.01),
        'w_in': nrm(ks[2], (d, IN_COLS), d ** -0.5),
        'cmp_k_pos': nrm(ks[3], (CMP_BLOCK, HEAD_DIM), 0.02),
        'cmp_k_w1': nrm(ks[4], (CMP_BLOCK * HEAD_DIM, CMP_HIDDEN), (CMP_BLOCK * HEAD_DIM) ** -0.5),
        'cmp_k_w2': nrm(ks[5], (CMP_HIDDEN, HEAD_DIM), CMP_HIDDEN ** -0.5),
        'cmp_v_pos': nrm(ks[6], (CMP_BLOCK, HEAD_DIM), 0.02),
        'cmp_v_w1': nrm(ks[7], (CMP_BLOCK * HEAD_DIM, CMP_HIDDEN), (CMP_BLOCK * HEAD_DIM) ** -0.5),
        'cmp_v_w2': nrm(ks[8], (CMP_HIDDEN, HEAD_DIM), CMP_HIDDEN ** -0.5),
        'conv_w': nrm(ks[9], (CONV_WIDTH, 1, CONV_DIM), CONV_WIDTH ** -0.5),
        'conv_b': nrm(ks[10], (CONV_DIM,), 0.01),
        'conv_norm_g': 1.0 + nrm(ks[11], (CONV_DIM,), 0.01),
        'conv_norm_b': nrm(ks[12], (CONV_DIM,), 0.01),
        'w_conv_out': nrm(ks[13], (CONV_DIM, d), CONV_DIM ** -0.5),
        'b_conv_out': nrm(ks[14], (d,), 0.01),
        'w_out': nrm(ks[15], (d, d), d ** -0.5),
        'norm_ffn_g': 1.0 + nrm(ks[16], (d,), 0.01),
        'w_ffn_gate': nrm(ks[17], (d, D_FF), d ** -0.5),
        'w_ffn_up': nrm(ks[18], (d, D_FF), d ** -0.5),
        'w_ffn_down': nrm(ks[19], (D_FF, d), D_FF ** -0.5),
        'norm_final_g': 1.0 + nrm(ks[20], (d,), 0.01),
    }


def reference(x, positions, norm_mix_g, w_in, cmp_k_pos, cmp_k_w1, cmp_k_w2, cmp_v_pos, cmp_v_w1, cmp_v_w2,
              conv_w, conv_b, conv_norm_g, conv_norm_b, w_conv_out, b_conv_out, w_out,
              norm_ffn_g, w_ffn_gate, w_ffn_up, w_ffn_down, norm_final_g):
    B, S, _ = x.shape
    cos, sin = rope_tables(positions, x.dtype)
    split_idx = [int(v) for v in np.cumsum(SPLIT_SIZES)[:-1]]
    for _layer in range(DEPTH):
        h = rms_norm(x, norm_mix_g)
        parts = jnp.split(h @ w_in, split_idx, axis=-1)
        (q, kc, vc, ks_, vs_, kw, vw, g_nsa, u_a, u_b, g_attn, g_conv) = parts
        q = apply_partial_rope(q.reshape(B, S, N_HEADS, HEAD_DIM), cos, sin)
        q = q.reshape(B, S, N_KV, GROUP, HEAD_DIM).transpose(0, 2, 3, 1, 4)
        rk = lambda t: apply_partial_rope(t.reshape(B, S, N_KV, HEAD_DIM), cos, sin).transpose(0, 2, 1, 3)
        rv = lambda t: t.reshape(B, S, N_KV, HEAD_DIM).transpose(0, 2, 1, 3)
        k_c = compress_blocks(rk(kc), cmp_k_pos, cmp_k_w1, cmp_k_w2)
        v_c = compress_blocks(rv(vc), cmp_v_pos, cmp_v_w1, cmp_v_w2)
        gates = jax.nn.sigmoid(g_nsa).reshape(B, S, 3, N_KV, GROUP).transpose(0, 3, 4, 1, 2)
        attn = nsa_attention(q, k_c, v_c, rk(ks_), rv(vs_), rk(kw), rv(vw), gates)
        conv = conformer_conv(u_a, u_b, conv_w, conv_b, conv_norm_g, conv_norm_b, w_conv_out, b_conv_out)
        merged = jax.nn.sigmoid(g_attn) * attn + jax.nn.sigmoid(g_conv) * conv
        x = x + merged @ w_out
        h = rms_norm(x, norm_ffn_g)
        x = x + (jax.nn.silu(h @ w_ffn_gate) * (h @ w_ffn_up)) @ w_ffn_down
    return rms_norm(x, norm_final_g)
```

```python
import functools

import jax
import jax.numpy as jnp
from jax import lax
from jax.experimental import pallas as pl
from jax.experimental.pallas import tpu as pltpu

F32 = jnp.float32
BF16 = jnp.bfloat16

D_MODEL = 1024
N_HEADS = 16
HEAD_DIM = 64
N_KV = 4
GROUP = N_HEADS // N_KV
ROPE_DIM = HEAD_DIM // 4
ROPE_THETA = 500000.0
CMP_BLOCK = 32
CMP_STRIDE = 16
CMP_HIDDEN = 4 * HEAD_DIM
SEL_BLOCK = 64
SEL_TOPK = 16
WINDOW = 512
CONV_WIDTH = 31
D_FF = ((8 * D_MODEL // 3 + 255) // 256) * 256
EPS = 1e-6

LANES = 128
KV_COLS = N_KV * HEAD_DIM
TQ = 128
QROWS = GROUP * TQ
KT = 512
WIN_KEYS = WINDOW + TQ
N_FORCED = 3
MASK_BIG = 2.0 ** 100
CONV_HALO = 32
VMEM_LIMIT = 56 * 1024 * 1024


def _sigmoid(v):
    return 1.0 / (1.0 + jnp.exp(-v))


def _const_spec(shape):
    nd = len(shape)
    return pl.BlockSpec(shape, lambda *_: (0,) * nd, pipeline_mode=pl.Buffered(1))


def _proj_kernel(x_ref, g_ref, cos_ref, sin_ref, wq_ref, wkv_ref, wgn_ref, wua_ref, wub_ref, wga_ref,
                 wgc_ref, q_ref, kc_ref, vc_ref, kst_ref, vs_ref, kwt_ref, vw_ref, gates_ref, z_ref,
                 ga_ref, gc_ref):
    x = x_ref[0]
    h = (x * lax.rsqrt(jnp.mean(x * x, axis=-1, keepdims=True) + EPS) * g_ref[...]).astype(BF16)
    cosf = cos_ref[0]
    sinf = sin_ref[0]

    def rope(y):
        w = y.shape[1]
        lane = lax.broadcasted_iota(jnp.int32, y.shape, 1)
        partner = jnp.where((lane & (HEAD_DIM - 1)) < ROPE_DIM // 2,
                            pltpu.roll(y, w - ROPE_DIM // 2, 1), pltpu.roll(y, ROPE_DIM // 2, 1))
        return y * pltpu.repeat(cosf, w // LANES, 1) + partner * pltpu.repeat(sinf, w // LANES, 1)

    def with_ones_col(v_pair, odd):
        lane = lax.broadcasted_iota(jnp.int32, v_pair.shape, 1)
        v = pltpu.roll(v_pair, HEAD_DIM, 1) if odd else v_pair
        return jnp.where(lane < HEAD_DIM, v, jnp.where(lane == HEAD_DIM, 1.0, 0.0)).astype(BF16)

    yq = jnp.dot(h, wq_ref[...], preferred_element_type=F32)
    q_ref[0] = (rope(yq) * (HEAD_DIM ** -0.5)).astype(BF16)

    ykv = jnp.dot(h, wkv_ref[...], preferred_element_type=F32)
    kc = rope(ykv[:, 0 * KV_COLS:1 * KV_COLS])
    vc = ykv[:, 1 * KV_COLS:2 * KV_COLS]
    for g in range(N_KV):
        kc_ref[0, g] = kc[:, g * HEAD_DIM:(g + 1) * HEAD_DIM]
        vc_ref[0, g] = vc[:, g * HEAD_DIM:(g + 1) * HEAD_DIM]
    kst_ref[0] = rope(ykv[:, 2 * KV_COLS:3 * KV_COLS]).T.astype(BF16)
    kwt_ref[0] = rope(ykv[:, 4 * KV_COLS:5 * KV_COLS]).T.astype(BF16)
    vs = ykv[:, 3 * KV_COLS:4 * KV_COLS]
    vw = ykv[:, 5 * KV_COLS:6 * KV_COLS]
    for g in range(N_KV):
        pair = slice((g // 2) * LANES, (g // 2 + 1) * LANES)
        vs_ref[0, g] = with_ones_col(vs[:, pair], g % 2 == 1)
        vw_ref[0, g] = with_ones_col(vw[:, pair], g % 2 == 1)

    gates = _sigmoid(jnp.dot(h, wgn_ref[...], preferred_element_type=F32))
    for g in range(N_KV):
        gates_ref[0, g] = gates[:, g * LANES:(g + 1) * LANES]

    ua = jnp.dot(h, wua_ref[...], preferred_element_type=F32)
    ub = jnp.dot(h, wub_ref[...], preferred_element_type=F32)
    z_ref[0] = ua * _sigmoid(ub)
    ga_ref[0] = _sigmoid(jnp.dot(h, wga_ref[...], preferred_element_type=F32))
    gc_ref[0] = _sigmoid(jnp.dot(h, wgc_ref[...], preferred_element_type=F32))


def _proj(x, norm_g, cosf, sinf, wq, wkv, wgn, wua, wub, wga, wgc, tm):
    b, s, d = x.shape
    grid = (b, s // tm)
    row = lambda w: pl.BlockSpec((1, tm, w), lambda bi, i: (bi, i, 0))
    per_group = lambda w: pl.BlockSpec((1, N_KV, tm, w), lambda bi, i: (bi, 0, i, 0))
    transposed = pl.BlockSpec((1, KV_COLS, tm), lambda bi, i: (bi, 0, i))
    out_shape = (
        jax.ShapeDtypeStruct((b, s, d), BF16),
        jax.ShapeDtypeStruct((b, N_KV, s, HEAD_DIM), F32),
        jax.ShapeDtypeStruct((b, N_KV, s, HEAD_DIM), F32),
        jax.ShapeDtypeStruct((b, KV_COLS, s), BF16),
        jax.ShapeDtypeStruct((b, N_KV, s, LANES), BF16),
        jax.ShapeDtypeStruct((b, KV_COLS, s), BF16),
        jax.ShapeDtypeStruct((b, N_KV, s, LANES), BF16),
        jax.ShapeDtypeStruct((b, N_KV, s, LANES), F32),
        jax.ShapeDtypeStruct((b, s, d), F32),
        jax.ShapeDtypeStruct((b, s, d), F32),
        jax.ShapeDtypeStruct((b, s, d), F32),
    )
    out_specs = (row(d), per_group(HEAD_DIM), per_group(HEAD_DIM), transposed, per_group(LANES),
                 transposed, per_group(LANES), per_group(LANES), row(d), row(d), row(d))
    in_specs = [row(d), _const_spec((1, d)), row(LANES), row(LANES),
                _const_spec(wq.shape), _const_spec(wkv.shape), _const_spec(wgn.shape),
                _const_spec(wua.shape), _const_spec(wub.shape), _const_spec(wga.shape),
                _const_spec(wgc.shape)]
    return pl.pallas_call(
        _proj_kernel, grid=grid, in_specs=in_specs, out_specs=out_specs, out_shape=out_shape,
        compiler_params=pltpu.CompilerParams(dimension_semantics=("parallel", "parallel"),
                                             vmem_limit_bytes=VMEM_LIMIT),
        name="proj",
    )(x, norm_g, cosf, sinf, wq, wkv, wgn, wua, wub, wga, wgc)


def _compress_kernel(c_ref, pos_ref, w1_ref, w2_ref, o_ref, *, transpose_out):
    half = CMP_STRIDE * HEAD_DIM
    c = c_ref[0, 0]
    top = (c + pos_ref[:, :half]).astype(BF16)
    bot = (c + pos_ref[:, half:]).astype(BF16)
    a = jnp.dot(top, w1_ref[:half, :], preferred_element_type=F32)
    bm = jnp.dot(bot, w1_ref[half:, :], preferred_element_type=F32)
    n = c.shape[0]
    hid = a + pltpu.roll(bm, n - 1, 0)
    hid = hid * _sigmoid(hid)
    out = jnp.dot(hid.astype(BF16), w2_ref[...], preferred_element_type=F32)
    if transpose_out:
        o_ref[0, 0] = out.T.astype(BF16)
    else:
        lane = lax.broadcasted_iota(jnp.int32, (n, HEAD_DIM), 1)
        ones_col = jnp.where(lane == 0, 1.0, 0.0)
        o_ref[0, 0] = jnp.concatenate([out, ones_col], axis=1).astype(BF16)


def _compress(chunks, pos_flat, w1, w2, transpose_out):
    b, g, n, width = chunks.shape
    if transpose_out:
        out_shape = jax.ShapeDtypeStruct((b, g, HEAD_DIM, n), BF16)
        out_spec = pl.BlockSpec((1, 1, HEAD_DIM, n), lambda bi, gi: (bi, gi, 0, 0))
    else:
        out_shape = jax.ShapeDtypeStruct((b, g, n, LANES), BF16)
        out_spec = pl.BlockSpec((1, 1, n, LANES), lambda bi, gi: (bi, gi, 0, 0))
    return pl.pallas_call(
        functools.partial(_compress_kernel, transpose_out=transpose_out),
        grid=(b, g),
        in_specs=[pl.BlockSpec((1, 1, n, width), lambda bi, gi: (bi, gi, 0, 0)),
                  _const_spec(pos_flat.shape), _const_spec(w1.shape), _const_spec(w2.shape)],
        out_specs=out_spec, out_shape=out_shape,
        compiler_params=pltpu.CompilerParams(dimension_semantics=("parallel", "parallel"),
                                             vmem_limit_bytes=VMEM_LIMIT),
        name="compress_k" if transpose_out else "compress_v",
    )(chunks, pos_flat, w1, w2)


def _nsa_kernel(q_ref, kct_ref, vc_ref, kst_ref, vs_ref, kwt_ref, vw_ref, gates_ref, ov_ref, nege_ref,
                o_ref, qaug_ref, m_ref, acc_ref):
    t0 = pl.program_id(2) * TQ
    n_cmp = kct_ref.shape[-1]

    qb = q_ref[0]
    for r in range(GROUP):
        qaug_ref[r * TQ:(r + 1) * TQ, LANES:LANES + HEAD_DIM] = qb[:, r * HEAD_DIM:(r + 1) * HEAD_DIM]
    q4 = qaug_ref[:, LANES:LANES + HEAD_DIM]

    def row_token(shape):
        return t0 + (lax.broadcasted_iota(jnp.int32, shape, 0) & (TQ - 1))

    s = jnp.dot(q4, kct_ref[0, 0], preferred_element_type=F32)
    col = lax.broadcasted_iota(jnp.int32, s.shape, 1)
    s = jnp.where(col * CMP_STRIDE + (CMP_BLOCK - 1) <= row_token(s.shape), s, -jnp.inf)
    m = jnp.max(s, axis=1, keepdims=True)
    m = jnp.where(m == -jnp.inf, 0.0, m)
    e = jnp.exp(s - m)
    d = jnp.sum(e, axis=1, keepdims=True)
    dinv = 1.0 / jnp.where(d > 0, d, 1.0)
    p = e * dinv
    o_cmp = jnp.dot(e.astype(BF16), vc_ref[0, 0], preferred_element_type=F32) * dinv

    psum = p[0:TQ] + p[TQ:2 * TQ] + p[2 * TQ:3 * TQ] + p[3 * TQ:4 * TQ]
    p_hi = psum.astype(BF16)
    p_lo = (psum - p_hi.astype(F32)).astype(BF16)
    imp = (jnp.dot(p_hi, ov_ref[...], preferred_element_type=F32)
           + jnp.dot(p_lo, ov_ref[...], preferred_element_type=F32))
    imp_t = imp.T

    j_i = lax.broadcasted_iota(jnp.int32, imp_t.shape, 0)
    qblk = (t0 + lax.broadcasted_iota(jnp.int32, imp_t.shape, 1)) // SEL_BLOCK
    forced = jnp.where(j_i == 0, 1.0, jnp.where(j_i == qblk, 1.0, jnp.where(j_i == qblk - 1, 1.0, 0.0)))
    work = jnp.where(j_i >= 1, jnp.where(j_i <= qblk - 2, imp_t, -1.0), -1.0)
    j_f = j_i.astype(F32)

    def pick_next(_, carry):
        work, taken = carry
        mx = jnp.max(work, axis=0, keepdims=True)
        idx = jnp.min(jnp.where(work == mx, j_f, 1e9), axis=0, keepdims=True)
        pick = j_f == jnp.where(mx >= 0.0, idx, -1.0)
        return jnp.where(pick, -1.0, work), jnp.where(pick, 1.0, taken)

    _, taken = lax.fori_loop(0, SEL_TOPK - N_FORCED, pick_next, (work, forced))
    not_sel = jnp.where(taken > 0.0, 0.0, MASK_BIG).T.astype(BF16)
    for r in range(GROUP):
        qaug_ref[r * TQ:(r + 1) * TQ, 0:LANES] = not_sel

    m_ref[...] = jnp.full(m_ref.shape, -jnp.inf, F32)
    acc_ref[...] = jnp.zeros(acc_ref.shape, F32)

    def sel_tile(c, carry):
        k0 = pl.multiple_of(c * KT, KT)
        kaug = jnp.concatenate([nege_ref[:, pl.ds(k0, KT)], kst_ref[0, :, pl.ds(k0, KT)]], axis=0)
        sc = jnp.dot(qaug_ref[...], kaug, preferred_element_type=F32)
        kpos = k0 + lax.broadcasted_iota(jnp.int32, sc.shape, 1)
        sc = jnp.where(kpos <= row_token(sc.shape), sc, -MASK_BIG)
        m_old = m_ref[...]
        m_new = jnp.maximum(m_old, jnp.max(sc, axis=1, keepdims=True))
        pr = jnp.exp(sc - pltpu.repeat(m_new, KT // LANES, 1))
        acc_ref[...] = (jnp.exp(m_old - m_new) * acc_ref[...]
                        + jnp.dot(pr.astype(BF16), vs_ref[0, 0, pl.ds(k0, KT), :], preferred_element_type=F32))
        m_ref[...] = m_new
        return carry

    lax.fori_loop(0, (t0 + TQ + KT - 1) // KT, sel_tile, 0)
    acc = acc_ref[...]
    o_sel = acc * (1.0 / acc[:, HEAD_DIM:HEAD_DIM + 1])

    w0 = pl.multiple_of(jnp.maximum(t0 - WINDOW, 0), LANES)
    sw = jnp.dot(q4, kwt_ref[0, :, pl.ds(w0, WIN_KEYS)], preferred_element_type=F32)
    kpos = w0 + lax.broadcasted_iota(jnp.int32, sw.shape, 1)
    tok = row_token(sw.shape)
    sw = jnp.where(kpos <= tok, jnp.where(kpos > tok - WINDOW, sw, -jnp.inf), -jnp.inf)
    ew = jnp.exp(sw - jnp.max(sw, axis=1, keepdims=True))
    o_win = jnp.dot(ew.astype(BF16), vw_ref[0, 0, pl.ds(w0, WIN_KEYS), :], preferred_element_type=F32)
    o_win = o_win * (1.0 / o_win[:, HEAD_DIM:HEAD_DIM + 1])

    gt = gates_ref[0, 0]
    for r in range(GROUP):
        rows = slice(r * TQ, (r + 1) * TQ)
        o = (gt[:, r:r + 1] * o_cmp[rows, :HEAD_DIM]
             + gt[:, GROUP + r:GROUP + r + 1] * o_sel[rows, :HEAD_DIM]
             + gt[:, 2 * GROUP + r:2 * GROUP + r + 1] * o_win[rows, :HEAD_DIM])
        o_ref[0, :, r * HEAD_DIM:(r + 1) * HEAD_DIM] = o


def _nsa(q, kct, vc, kst, vs, kwt, vw, gates, ov, nege):
    b, s, d = q.shape
    n_cmp = kct.shape[-1]
    grid = (b, N_KV, s // TQ)
    per_bg = lambda shape: pl.BlockSpec((1, 1) + shape, lambda bi, gi, i: (bi, gi, 0, 0))
    kt_spec = pl.BlockSpec((1, HEAD_DIM, s), lambda bi, gi, i: (bi, gi, 0))
    in_specs = [
        pl.BlockSpec((1, TQ, GROUP * HEAD_DIM), lambda bi, gi, i: (bi, i, gi)),
        per_bg((HEAD_DIM, n_cmp)), per_bg((n_cmp, LANES)),
        kt_spec, per_bg((s, LANES)), kt_spec, per_bg((s, LANES)),
        pl.BlockSpec((1, 1, TQ, LANES), lambda bi, gi, i: (bi, gi, i, 0)),
        pl.BlockSpec(ov.shape, lambda bi, gi, i: (0, 0)),
        pl.BlockSpec(nege.shape, lambda bi, gi, i: (0, 0)),
    ]
    return pl.pallas_call(
        _nsa_kernel, grid=grid, in_specs=in_specs,
        out_specs=pl.BlockSpec((1, TQ, GROUP * HEAD_DIM), lambda bi, gi, i: (bi, i, gi)),
        out_shape=jax.ShapeDtypeStruct((b, s, d), F32),
        scratch_shapes=[pltpu.VMEM((QROWS, LANES + HEAD_DIM), BF16),
                        pltpu.VMEM((QROWS, LANES), F32),
                        pltpu.VMEM((QROWS, LANES), F32)],
        compiler_params=pltpu.CompilerParams(dimension_semantics=("parallel", "parallel", "arbitrary"),
                                             vmem_limit_bytes=VMEM_LIMIT),
        name="nsa",
    )(q, kct, vc, kst, vs, kwt, vw, gates, ov, nege)


def _mix_kernel(x_ref, z_ref, zprev_ref, attn_ref, ga_ref, gc_ref, cw_ref, cb_ref, lg_ref, lb_ref,
                wco_ref, bco_ref, wout_ref, o_ref, zbuf_ref):
    tm = z_ref.shape[1]
    halo = zprev_ref[0]
    zbuf_ref[0:CONV_HALO, :] = jnp.where(pl.program_id(1) > 0, halo, jnp.zeros_like(halo))
    zbuf_ref[CONV_HALO:CONV_HALO + tm, :] = z_ref[0]
    first = CONV_HALO - (CONV_WIDTH - 1)
    acc = cw_ref[0:1, :] * zbuf_ref[first:first + tm, :]
    for k in range(1, CONV_WIDTH):
        acc = acc + cw_ref[k:k + 1, :] * zbuf_ref[first + k:first + k + tm, :]
    zc = acc + cb_ref[...]
    mu = jnp.mean(zc, axis=-1, keepdims=True)
    dev = zc - mu
    var = jnp.mean(dev * dev, axis=-1, keepdims=True)
    zn = dev * lax.rsqrt(var + EPS) * lg_ref[...] + lb_ref[...]
    act = (zn * _sigmoid(zn)).astype(BF16)
    conv = jnp.dot(act, wco_ref[...], preferred_element_type=F32) + bco_ref[...]
    merged = (ga_ref[0] * attn_ref[0] + gc_ref[0] * conv).astype(BF16)
    o_ref[0] = x_ref[0] + jnp.dot(merged, wout_ref[...], preferred_element_type=F32)


def _mix(x, z, attn, ga, gc, conv_w, conv_b, ln_g, ln_b, w_co, b_co, w_out, tm):
    b, s, d = x.shape
    row = pl.BlockSpec((1, tm, d), lambda bi, i: (bi, i, 0))
    ratio = tm // CONV_HALO
    prev = pl.BlockSpec((1, CONV_HALO, d), lambda bi, i: (bi, jnp.maximum(i * ratio - 1, 0), 0))
    in_specs = [row, row, prev, row, row, row, _const_spec(conv_w.shape), _const_spec((1, d)),
                _const_spec((1, d)), _const_spec((1, d)), _const_spec(w_co.shape), _const_spec((1, d)),
                _const_spec(w_out.shape)]
    return pl.pallas_call(
        _mix_kernel, grid=(b, s // tm), in_specs=in_specs, out_specs=row,
        out_shape=jax.ShapeDtypeStruct((b, s, d), F32),
        scratch_shapes=[pltpu.VMEM((CONV_HALO + tm, d), F32)],
        compiler_params=pltpu.CompilerParams(dimension_semantics=("parallel", "parallel"),
                                             vmem_limit_bytes=VMEM_LIMIT),
        name="mix",
    )(x, z, z, attn, ga, gc, conv_w, conv_b, ln_g, ln_b, w_co, b_co, w_out)


def _ffn_kernel(x_ref, g_ref, wg_ref, wu_ref, wd_ref, gf_ref, o_ref, *, chunk):
    x = x_ref[...]
    h = (x * lax.rsqrt(jnp.mean(x * x, axis=-1, keepdims=True) + EPS) * g_ref[...]).astype(BF16)
    y = x
    for c in range(D_FF // chunk):
        cols = slice(c * chunk, (c + 1) * chunk)
        gate = jnp.dot(h, wg_ref[:, cols], preferred_element_type=F32)
        up = jnp.dot(h, wu_ref[:, cols], preferred_element_type=F32)
        act = (gate * _sigmoid(gate) * up).astype(BF16)
        y = y + jnp.dot(act, wd_ref[cols, :], preferred_element_type=F32)
    o_ref[...] = y * lax.rsqrt(jnp.mean(y * y, axis=-1, keepdims=True) + EPS) * gf_ref[...]


def _ffn(x, norm_g, w_gate, w_up, w_down, final_g, tm, chunk):
    t, d = x.shape
    row = pl.BlockSpec((tm, d), lambda i: (i, 0))
    in_specs = [row, _const_spec((1, d)), _const_spec(w_gate.shape), _const_spec(w_up.shape),
                _const_spec(w_down.shape), _const_spec((1, d))]
    return pl.pallas_call(
        functools.partial(_ffn_kernel, chunk=chunk), grid=(t // tm,), in_specs=in_specs, out_specs=row,
        out_shape=jax.ShapeDtypeStruct((t, d), F32),
        compiler_params=pltpu.CompilerParams(dimension_semantics=("parallel",),
                                             vmem_limit_bytes=VMEM_LIMIT),
        name="ffn",
    )(x, norm_g, w_gate, w_up, w_down, final_g)


def _rope_tables(positions):
    half = ROPE_DIM // 2
    inv = ROPE_THETA ** (-jnp.arange(0, ROPE_DIM, 2, dtype=F32) / ROPE_DIM)
    ang = positions.astype(F32)[..., None] * inv
    cos, sin = jnp.cos(ang), jnp.sin(ang)
    rest = positions.shape + (HEAD_DIM - 2 * half,)
    cos_h = jnp.concatenate([cos, cos, jnp.ones(rest, F32)], axis=-1)
    sin_h = jnp.concatenate([-sin, sin, jnp.zeros(rest, F32)], axis=-1)
    reps = LANES // HEAD_DIM
    return jnp.tile(cos_h, (1, 1, reps)), jnp.tile(sin_h, (1, 1, reps))


def _selection_constants(s):
    n_cmp = s // CMP_STRIDE
    ks = jnp.arange(n_cmp)[:, None] * CMP_STRIDE
    bs = jnp.arange(LANES)[None, :] * SEL_BLOCK
    overlap = jnp.clip(jnp.minimum(ks + CMP_BLOCK, bs + SEL_BLOCK) - jnp.maximum(ks, bs), 0)
    overlap = jnp.where(bs < s, overlap, 0).astype(F32) / CMP_STRIDE
    neg_e = -(jnp.arange(s)[None, :] // SEL_BLOCK == jnp.arange(LANES)[:, None]).astype(F32)
    return overlap.astype(BF16), neg_e.astype(BF16)


def kernel(x, positions, norm_mix_g, w_in, cmp_k_pos, cmp_k_w1, cmp_k_w2, cmp_v_pos, cmp_v_w1, cmp_v_w2,
           conv_w, conv_b, conv_norm_g, conv_norm_b, w_conv_out, b_conv_out, w_out,
           norm_ffn_g, w_ffn_gate, w_ffn_up, w_ffn_down, norm_final_g):
    b, s, d = x.shape
    assert d == D_MODEL and s % KT == 0 and s // SEL_BLOCK <= LANES and s >= WIN_KEYS
    row_vec = lambda v: v.reshape(1, -1).astype(F32)

    o_kv = N_HEADS * HEAD_DIM
    o_gn = o_kv + 6 * KV_COLS
    o_ua = o_gn + 3 * N_HEADS
    wq = w_in[:, :o_kv].astype(BF16)
    wkv = w_in[:, o_kv:o_gn].astype(BF16)
    wgn = w_in[:, o_gn:o_ua].reshape(d, 3, N_KV, GROUP).transpose(0, 2, 1, 3).reshape(d, N_KV, 3 * GROUP)
    wgn = jnp.pad(wgn, ((0, 0), (0, 0), (0, LANES - 3 * GROUP))).reshape(d, N_KV * LANES).astype(BF16)
    wua, wub, wga, wgc = (w_in[:, o_ua + k * d:o_ua + (k + 1) * d].astype(BF16) for k in range(4))

    cosf, sinf = _rope_tables(positions)
    q, kc, vc, kst, vs, kwt, vw, gates, z, ga, gc = _proj(
        x, row_vec(norm_mix_g), cosf, sinf, wq, wkv, wgn, wua, wub, wga, wgc, tm=512)

    n_chunks = s // CMP_STRIDE
    chunk_w = CMP_STRIDE * HEAD_DIM
    kct = _compress(kc.reshape(b, N_KV, n_chunks, chunk_w), cmp_k_pos.reshape(1, -1),
                    cmp_k_w1.astype(BF16), cmp_k_w2.astype(BF16), True)
    vcp = _compress(vc.reshape(b, N_KV, n_chunks, chunk_w), cmp_v_pos.reshape(1, -1),
                    cmp_v_w1.astype(BF16), cmp_v_w2.astype(BF16), False)

    ov, nege = _selection_constants(s)
    attn = _nsa(q, kct, vcp, kst, vs, kwt, vw, gates, ov, nege)

    x1 = _mix(x, z, attn, ga, gc, conv_w.reshape(CONV_WIDTH, d), row_vec(conv_b), row_vec(conv_norm_g),
              row_vec(conv_norm_b), w_conv_out.astype(BF16), row_vec(b_conv_out), w_out.astype(BF16), tm=256)

    out = _ffn(x1.reshape(b * s, d), row_vec(norm_ffn_g), w_ffn_gate.astype(BF16), w_ffn_up.astype(BF16),
               w_ffn_down.astype(BF16), row_vec(norm_final_g), tm=512, chunk=D_FF // 2)
    return out.reshape(b, s, d)
```

```python
import functools
import math

import jax
import jax.numpy as jnp
from jax import lax
from jax.experimental import pallas as pl
from jax.experimental.pallas import tpu as pltpu

F32 = jnp.float32
BF16 = jnp.bfloat16

D_MODEL = 1024
N_HEADS = 16
HEAD_DIM = 64
N_KV = 4
GROUP = N_HEADS // N_KV
ROPE_DIM = HEAD_DIM // 4
ROPE_THETA = 500000.0
CMP_BLOCK = 32
CMP_STRIDE = 16
CMP_HIDDEN = 4 * HEAD_DIM
SEL_BLOCK = 64
SEL_TOPK = 16
WINDOW = 512
CONV_WIDTH = 31
D_FF = ((8 * D_MODEL // 3 + 255) // 256) * 256
EPS = 1e-6

LANES = 128
SUBLANES = 8
KV_COLS = N_KV * HEAD_DIM
TQ = LANES
QROWS = GROUP * TQ
KT = 512
WIN_KEYS = WINDOW + TQ
N_FORCED = 3
MASK_BIG = 2.0 ** 100
Q_SCALE = HEAD_DIM ** -0.5 * math.log2(math.e)
CONV_HALO = 32
VMEM_LIMIT = 56 * 1024 * 1024


def _sigmoid(v):
    return 1.0 / (1.0 + jnp.exp(-v))


def _const_spec(shape):
    nd = len(shape)
    return pl.BlockSpec(shape, lambda *_: (0,) * nd, pipeline_mode=pl.Buffered(1))


def _proj_kernel(x_ref, g_ref, cos_ref, sin_ref, wq_ref, wkv_ref, wgn_ref, wua_ref, wub_ref, wga_ref,
                 wgc_ref, q_ref, kc_ref, vc_ref, kst_ref, vs_ref, kwt_ref, vw_ref, gates_ref, z_ref,
                 ga_ref, gc_ref):
    x = x_ref[0]
    h = (x * lax.rsqrt(jnp.mean(x * x, axis=-1, keepdims=True) + EPS) * g_ref[...]).astype(BF16)
    cosf = cos_ref[0]
    sinf = sin_ref[0]

    def rope(y):
        w = y.shape[1]
        lane = lax.broadcasted_iota(jnp.int32, y.shape, 1)
        partner = jnp.where((lane & (HEAD_DIM - 1)) < ROPE_DIM // 2,
                            pltpu.roll(y, w - ROPE_DIM // 2, 1), pltpu.roll(y, ROPE_DIM // 2, 1))
        return y * pltpu.repeat(cosf, w // LANES, 1) + partner * pltpu.repeat(sinf, w // LANES, 1)

    def with_ones_col(v_pair, odd):
        lane = lax.broadcasted_iota(jnp.int32, v_pair.shape, 1)
        v = pltpu.roll(v_pair, HEAD_DIM, 1) if odd else v_pair
        return jnp.where(lane < HEAD_DIM, v, jnp.where(lane == HEAD_DIM, 1.0, 0.0)).astype(BF16)

    yq = jnp.dot(h, wq_ref[...], preferred_element_type=F32)
    q_ref[0] = (rope(yq) * Q_SCALE).astype(BF16)

    ykv = jnp.dot(h, wkv_ref[...], preferred_element_type=F32)
    kc = rope(ykv[:, 0 * KV_COLS:1 * KV_COLS])
    vc = ykv[:, 1 * KV_COLS:2 * KV_COLS]
    for g in range(N_KV):
        kc_ref[0, g] = kc[:, g * HEAD_DIM:(g + 1) * HEAD_DIM]
        vc_ref[0, g] = vc[:, g * HEAD_DIM:(g + 1) * HEAD_DIM]
    kst_ref[0] = rope(ykv[:, 2 * KV_COLS:3 * KV_COLS]).T.astype(BF16)
    kwt_ref[0] = rope(ykv[:, 4 * KV_COLS:5 * KV_COLS]).T.astype(BF16)
    vs = ykv[:, 3 * KV_COLS:4 * KV_COLS]
    vw = ykv[:, 5 * KV_COLS:6 * KV_COLS]
    for g in range(N_KV):
        pair = slice((g // 2) * LANES, (g // 2 + 1) * LANES)
        vs_ref[0, g] = with_ones_col(vs[:, pair], g % 2 == 1)
        vw_ref[0, g] = with_ones_col(vw[:, pair], g % 2 == 1)

    gates = _sigmoid(jnp.dot(h, wgn_ref[...], preferred_element_type=F32))
    for g in range(N_KV):
        gates_ref[0, g] = gates[:, g * LANES:(g + 1) * LANES]

    ua = jnp.dot(h, wua_ref[...], preferred_element_type=F32)
    ub = jnp.dot(h, wub_ref[...], preferred_element_type=F32)
    z_ref[0] = ua * _sigmoid(ub)
    ga_ref[0] = _sigmoid(jnp.dot(h, wga_ref[...], preferred_element_type=F32))
    gc_ref[0] = _sigmoid(jnp.dot(h, wgc_ref[...], preferred_element_type=F32))


def _proj(x, norm_g, cosf, sinf, wq, wkv, wgn, wua, wub, wga, wgc, tm):
    b, s, d = x.shape
    grid = (b, s // tm)
    row = lambda w: pl.BlockSpec((1, tm, w), lambda bi, i: (bi, i, 0))
    per_group = lambda w: pl.BlockSpec((1, N_KV, tm, w), lambda bi, i: (bi, 0, i, 0))
    transposed = pl.BlockSpec((1, KV_COLS, tm), lambda bi, i: (bi, 0, i))
    out_shape = (
        jax.ShapeDtypeStruct((b, s, d), BF16),
        jax.ShapeDtypeStruct((b, N_KV, s, HEAD_DIM), F32),
        jax.ShapeDtypeStruct((b, N_KV, s, HEAD_DIM), F32),
        jax.ShapeDtypeStruct((b, KV_COLS, s), BF16),
        jax.ShapeDtypeStruct((b, N_KV, s, LANES), BF16),
        jax.ShapeDtypeStruct((b, KV_COLS, s), BF16),
        jax.ShapeDtypeStruct((b, N_KV, s, LANES), BF16),
        jax.ShapeDtypeStruct((b, N_KV, s, LANES), F32),
        jax.ShapeDtypeStruct((b, s, d), F32),
        jax.ShapeDtypeStruct((b, s, d), F32),
        jax.ShapeDtypeStruct((b, s, d), F32),
    )
    out_specs = (row(d), per_group(HEAD_DIM), per_group(HEAD_DIM), transposed, per_group(LANES),
                 transposed, per_group(LANES), per_group(LANES), row(d), row(d), row(d))
    in_specs = [row(d), _const_spec((1, d)), row(LANES), row(LANES),
                _const_spec(wq.shape), _const_spec(wkv.shape), _const_spec(wgn.shape),
                _const_spec(wua.shape), _const_spec(wub.shape), _const_spec(wga.shape),
                _const_spec(wgc.shape)]
    return pl.pallas_call(
        _proj_kernel, grid=grid, in_specs=in_specs, out_specs=out_specs, out_shape=out_shape,
        compiler_params=pltpu.CompilerParams(dimension_semantics=("parallel", "parallel"),
                                             vmem_limit_bytes=VMEM_LIMIT),
        name="proj",
    )(x, norm_g, cosf, sinf, wq, wkv, wgn, wua, wub, wga, wgc)


def _compress_kernel(c_ref, pos_ref, w1_ref, w2_ref, o_ref, *, transpose_out):
    half = CMP_STRIDE * HEAD_DIM
    c = c_ref[0, 0]
    top = (c + pos_ref[:, :half]).astype(BF16)
    bot = (c + pos_ref[:, half:]).astype(BF16)
    a = jnp.dot(top, w1_ref[:half, :], preferred_element_type=F32)
    bm = jnp.dot(bot, w1_ref[half:, :], preferred_element_type=F32)
    n = c.shape[0]
    hid = a + pltpu.roll(bm, n - 1, 0)
    hid = hid * _sigmoid(hid)
    out = jnp.dot(hid.astype(BF16), w2_ref[...], preferred_element_type=F32)
    if transpose_out:
        o_ref[0, 0] = out.T.astype(BF16)
    else:
        lane = lax.broadcasted_iota(jnp.int32, (n, HEAD_DIM), 1)
        ones_col = jnp.where(lane == 0, 1.0, 0.0)
        o_ref[0, 0] = jnp.concatenate([out, ones_col], axis=1).astype(BF16)


def _compress(chunks, pos_flat, w1, w2, transpose_out):
    b, g, n, width = chunks.shape
    if transpose_out:
        out_shape = jax.ShapeDtypeStruct((b, g, HEAD_DIM, n), BF16)
        out_spec = pl.BlockSpec((1, 1, HEAD_DIM, n), lambda bi, gi: (bi, gi, 0, 0))
    else:
        out_shape = jax.ShapeDtypeStruct((b, g, n, LANES), BF16)
        out_spec = pl.BlockSpec((1, 1, n, LANES), lambda bi, gi: (bi, gi, 0, 0))
    return pl.pallas_call(
        functools.partial(_compress_kernel, transpose_out=transpose_out),
        grid=(b, g),
        in_specs=[pl.BlockSpec((1, 1, n, width), lambda bi, gi: (bi, gi, 0, 0)),
                  _const_spec(pos_flat.shape), _const_spec(w1.shape), _const_spec(w2.shape)],
        out_specs=out_spec, out_shape=out_shape,
        compiler_params=pltpu.CompilerParams(dimension_semantics=("parallel", "parallel"),
                                             vmem_limit_bytes=VMEM_LIMIT),
        name="compress_k" if transpose_out else "compress_v",
    )(chunks, pos_flat, w1, w2)


def _nsa_kernel(q_ref, kct_ref, vc_ref, kst_ref, vs_ref, kwt_ref, vw_ref, gates_ref, ov_ref, nege_ref,
                cmask_ref, band_ref, tri_ref, o_ref, qi_ref, qs_ref, m_ref, l_ref, acc_ref, part_ref, gsel_ref,
                s0_ref, s1_ref):
    t0 = pl.multiple_of(pl.program_id(2) * TQ, TQ)

    qb = q_ref[0]
    eye = jnp.where(lax.broadcasted_iota(jnp.int32, (TQ, LANES), 0)
                    == lax.broadcasted_iota(jnp.int32, (TQ, LANES), 1), 1.0, 0.0).astype(BF16)
    for r in range(GROUP):
        rows = slice(r * TQ, (r + 1) * TQ)
        q_r = qb[:, r * HEAD_DIM:(r + 1) * HEAD_DIM]
        qi_ref[rows, 0:LANES] = eye
        qi_ref[rows, LANES:LANES + HEAD_DIM] = q_r
        qs_ref[rows, LANES:LANES + HEAD_DIM] = q_r
    qi = qi_ref[...]

    kaug = jnp.concatenate([cmask_ref[0], kct_ref[0, 0]], axis=0)
    s = jnp.dot(qi, kaug, preferred_element_type=F32)
    m = jnp.max(s, axis=1, keepdims=True)
    e = jnp.exp2(s - m)
    d = jnp.sum(e, axis=1, keepdims=True)
    dinv = jnp.where(m > -0.5 * MASK_BIG, 1.0 / d, 0.0)
    p = e * dinv
    o_cmp = jnp.dot(e.astype(BF16), vc_ref[0, 0], preferred_element_type=F32) * dinv

    psum = p[0:TQ] + p[TQ:2 * TQ] + p[2 * TQ:3 * TQ] + p[3 * TQ:4 * TQ]
    p_hi = psum.astype(BF16)
    p_lo = (psum - p_hi.astype(F32)).astype(BF16)
    imp = (jnp.dot(p_hi, ov_ref[...], preferred_element_type=F32)
           + jnp.dot(p_lo, ov_ref[...], preferred_element_type=F32))
    imp_t = imp.T

    j_i = lax.broadcasted_iota(jnp.int32, imp_t.shape, 0)
    qblk = (t0 + lax.broadcasted_iota(jnp.int32, imp_t.shape, 1)) // SEL_BLOCK
    taken = jnp.where(j_i == 0, 1.0, jnp.where(j_i == qblk - 1, 1.0, 0.0))
    work = jnp.where(j_i >= 1, jnp.where(j_i <= qblk - 2, imp_t, -1.0), -1.0)
    j_f = j_i.astype(F32)
    for _ in range(SEL_TOPK - N_FORCED):
        mx = jnp.max(work, axis=0, keepdims=True)
        idx = jnp.min(jnp.where(work == mx, j_f, 1e9), axis=0, keepdims=True)
        pick = j_f == jnp.where(mx >= 0.0, idx, -1.0)
        work = jnp.where(pick, -1.0, work)
        taken = jnp.where(pick, 1.0, taken)
    not_sel = jnp.where(j_i * SEL_BLOCK < t0, jnp.where(taken > 0.0, 0.0, MASK_BIG), MASK_BIG)
    not_sel = not_sel.T.astype(BF16)
    for r in range(GROUP):
        qs_ref[r * TQ:(r + 1) * TQ, 0:LANES] = not_sel

    w0 = pl.multiple_of(jnp.maximum(t0 - WINDOW, 0), LANES)
    shift = pl.multiple_of(jnp.maximum(WINDOW - t0, 0), LANES)
    kaug = jnp.concatenate([band_ref[:, pl.ds(shift, WIN_KEYS)], kwt_ref[0, :, pl.ds(w0, WIN_KEYS)]], axis=0)
    sw = jnp.dot(qi, kaug, preferred_element_type=F32)
    ew = jnp.exp2(sw - jnp.max(sw, axis=1, keepdims=True))
    o_win = jnp.dot(ew.astype(BF16), vw_ref[0, 0, pl.ds(w0, WIN_KEYS), :], preferred_element_type=F32)
    o_win = o_win * (1.0 / jnp.sum(ew, axis=1, keepdims=True))

    gt = gates_ref[0, 0]
    for r in range(GROUP):
        rows = slice(r * TQ, (r + 1) * TQ)
        part_ref[rows, :] = (gt[:, r:r + 1] * o_cmp[rows, :]
                             + gt[:, 2 * GROUP + r:2 * GROUP + r + 1] * o_win[rows, :])
        gsel_ref[rows, :] = jnp.broadcast_to(gt[:, GROUP + r:GROUP + r + 1], (TQ, LANES))

    kaug = jnp.concatenate([tri_ref[...], kst_ref[0, :, pl.ds(t0, TQ)]], axis=0)
    sd = jnp.dot(qi, kaug, preferred_element_type=F32)
    m0 = jnp.max(sd, axis=1, keepdims=True)
    pd = jnp.exp2(sd - m0)
    acc_ref[...] = jnp.dot(pd.astype(BF16), vs_ref[0, 0, pl.ds(t0, TQ), :], preferred_element_type=F32)
    m_ref[...] = jnp.broadcast_to(m0, m_ref.shape)
    l_ref[...] = jnp.broadcast_to(jnp.sum(pd, axis=1, keepdims=True), l_ref.shape)

    n_tiles = (t0 + KT - 1) // KT
    last_tile = kst_ref.shape[-1] // KT - 1

    def produce(s_ref, c):
        k0 = pl.multiple_of(jnp.minimum(c, last_tile) * KT, KT)
        kaug = jnp.concatenate([nege_ref[:, pl.ds(k0, KT)], kst_ref[0, :, pl.ds(k0, KT)]], axis=0)
        s_ref[...] = jnp.dot(qs_ref[...], kaug, preferred_element_type=F32)

    def consume(s_ref, c):
        k0 = pl.multiple_of(c * KT, KT)
        sc = s_ref[...]
        m_old = m_ref[...]
        m_new = jnp.maximum(m_old, jnp.max(sc, axis=1, keepdims=True))
        pr = jnp.exp2(sc - pltpu.repeat(m_new, KT // LANES, 1))
        alpha = jnp.exp2(m_old - m_new)
        l_ref[...] = alpha * l_ref[...] + jnp.sum(pr, axis=1, keepdims=True)
        acc_ref[...] = (alpha * acc_ref[...]
                        + jnp.dot(pr.astype(BF16), vs_ref[0, 0, pl.ds(k0, KT), :], preferred_element_type=F32))
        m_ref[...] = m_new

    produce(s0_ref, 0)

    def tile_pair(pair, carry):
        c = 2 * pair
        produce(s1_ref, c + 1)
        consume(s0_ref, c)
        produce(s0_ref, c + 2)
        consume(s1_ref, c + 1)
        return carry

    lax.fori_loop(0, n_tiles // 2, tile_pair, 0)

    @pl.when(n_tiles % 2 == 1)
    def _():
        consume(s0_ref, n_tiles - 1)

    o_all = part_ref[...] + gsel_ref[...] * acc_ref[...] * (1.0 / l_ref[...])
    for r in range(GROUP):
        o_ref[0, :, r * HEAD_DIM:(r + 1) * HEAD_DIM] = o_all[r * TQ:(r + 1) * TQ, :HEAD_DIM]


def _nsa(q, kct, vc, kst, vs, kwt, vw, gates, ov, nege, cmask, band, tri):
    b, s, d = q.shape
    n_cmp = kct.shape[-1]
    grid = (b, N_KV, s // TQ)
    per_bg = lambda shape: pl.BlockSpec((1, 1) + shape, lambda bi, gi, i: (bi, gi, 0, 0))
    kt_spec = pl.BlockSpec((1, HEAD_DIM, s), lambda bi, gi, i: (bi, gi, 0))
    whole = lambda a: pl.BlockSpec(a.shape, lambda bi, gi, i: (0,) * a.ndim)
    in_specs = [
        pl.BlockSpec((1, TQ, GROUP * HEAD_DIM), lambda bi, gi, i: (bi, i, gi)),
        per_bg((HEAD_DIM, n_cmp)), per_bg((n_cmp, LANES)),
        kt_spec, per_bg((s, LANES)), kt_spec, per_bg((s, LANES)),
        pl.BlockSpec((1, 1, TQ, LANES), lambda bi, gi, i: (bi, gi, i, 0)),
        whole(ov), whole(nege),
        pl.BlockSpec((1, TQ, n_cmp), lambda bi, gi, i: (i, 0, 0)),
        whole(band), whole(tri),
    ]
    return pl.pallas_call(
        _nsa_kernel, grid=grid, in_specs=in_specs,
        out_specs=pl.BlockSpec((1, TQ, GROUP * HEAD_DIM), lambda bi, gi, i: (bi, i, gi)),
        out_shape=jax.ShapeDtypeStruct((b, s, d), F32),
        scratch_shapes=[pltpu.VMEM((QROWS, LANES + HEAD_DIM), BF16),
                        pltpu.VMEM((QROWS, LANES + HEAD_DIM), BF16),
                        pltpu.VMEM((QROWS, LANES), F32),
                        pltpu.VMEM((QROWS, LANES), F32),
                        pltpu.VMEM((QROWS, LANES), F32),
                        pltpu.VMEM((QROWS, LANES), F32),
                        pltpu.VMEM((QROWS, LANES), F32),
                        pltpu.VMEM((QROWS, KT), F32),
                        pltpu.VMEM((QROWS, KT), F32)],
        compiler_params=pltpu.CompilerParams(dimension_semantics=("parallel", "parallel", "arbitrary"),
                                             vmem_limit_bytes=VMEM_LIMIT),
        name="nsa",
    )(q, kct, vc, kst, vs, kwt, vw, gates, ov, nege, cmask, band, tri)


def _mix_kernel(x_ref, z_ref, zprev_ref, attn_ref, ga_ref, gc_ref, cw_ref, cb_ref, lg_ref, lb_ref,
                wco_ref, bco_ref, wout_ref, o_ref, zbuf_ref, zshift_ref):
    tm = z_ref.shape[1]
    halo = zprev_ref[0]
    zbuf_ref[0:CONV_HALO, :] = jnp.where(pl.program_id(1) > 0, halo, jnp.zeros_like(halo))
    zbuf_ref[CONV_HALO:CONV_HALO + tm, :] = z_ref[0]
    first = CONV_HALO - (CONV_WIDTH - 1)
    zc = cb_ref[...]
    for phase in range(SUBLANES):
        taps = [k for k in range(CONV_WIDTH) if (first + k) % SUBLANES == phase]
        span = max(first + k - phase for k in taps) + tm
        zshift_ref[phase, 0:span, :] = zbuf_ref[phase:phase + span, :]
        for k in taps:
            off = first + k - phase
            zc = zc + cw_ref[k:k + 1, :] * zshift_ref[phase, off:off + tm, :]
    mu = jnp.mean(zc, axis=-1, keepdims=True)
    dev = zc - mu
    var = jnp.mean(dev * dev, axis=-1, keepdims=True)
    zn = dev * lax.rsqrt(var + EPS) * lg_ref[...] + lb_ref[...]
    act = (zn * _sigmoid(zn)).astype(BF16)
    conv = jnp.dot(act, wco_ref[...], preferred_element_type=F32) + bco_ref[...]
    merged = (ga_ref[0] * attn_ref[0] + gc_ref[0] * conv).astype(BF16)
    o_ref[0] = x_ref[0] + jnp.dot(merged, wout_ref[...], preferred_element_type=F32)


def _mix(x, z, attn, ga, gc, conv_w, conv_b, ln_g, ln_b, w_co, b_co, w_out, tm):
    b, s, d = x.shape
    row = pl.BlockSpec((1, tm, d), lambda bi, i: (bi, i, 0))
    ratio = tm // CONV_HALO
    prev = pl.BlockSpec((1, CONV_HALO, d), lambda bi, i: (bi, jnp.maximum(i * ratio - 1, 0), 0))
    in_specs = [row, row, prev, row, row, row, _const_spec(conv_w.shape), _const_spec((1, d)),
                _const_spec((1, d)), _const_spec((1, d)), _const_spec(w_co.shape), _const_spec((1, d)),
                _const_spec(w_out.shape)]
    return pl.pallas_call(
        _mix_kernel, grid=(b, s // tm), in_specs=in_specs, out_specs=row,
        out_shape=jax.ShapeDtypeStruct((b, s, d), F32),
        scratch_shapes=[pltpu.VMEM((CONV_HALO + tm, d), F32),
                        pltpu.VMEM((SUBLANES, CONV_HALO + tm, d), F32)],
        compiler_params=pltpu.CompilerParams(dimension_semantics=("parallel", "parallel"),
                                             vmem_limit_bytes=VMEM_LIMIT),
        name="mix",
    )(x, z, z, attn, ga, gc, conv_w, conv_b, ln_g, ln_b, w_co, b_co, w_out)


def _ffn_kernel(x_ref, g_ref, wg_ref, wu_ref, wd_ref, gf_ref, o_ref, *, chunk):
    x = x_ref[...]
    h = (x * lax.rsqrt(jnp.mean(x * x, axis=-1, keepdims=True) + EPS) * g_ref[...]).astype(BF16)
    y = x
    for c in range(D_FF // chunk):
        cols = slice(c * chunk, (c + 1) * chunk)
        gate = jnp.dot(h, wg_ref[:, cols], preferred_element_type=F32)
        up = jnp.dot(h, wu_ref[:, cols], preferred_element_type=F32)
        act = (gate * _sigmoid(gate) * up).astype(BF16)
        y = y + jnp.dot(act, wd_ref[cols, :], preferred_element_type=F32)
    o_ref[...] = y * lax.rsqrt(jnp.mean(y * y, axis=-1, keepdims=True) + EPS) * gf_ref[...]


def _ffn(x, norm_g, w_gate, w_up, w_down, final_g, tm, chunk):
    t, d = x.shape
    row = pl.BlockSpec((tm, d), lambda i: (i, 0))
    in_specs = [row, _const_spec((1, d)), _const_spec(w_gate.shape), _const_spec(w_up.shape),
                _const_spec(w_down.shape), _const_spec((1, d))]
    return pl.pallas_call(
        functools.partial(_ffn_kernel, chunk=chunk), grid=(t // tm,), in_specs=in_specs, out_specs=row,
        out_shape=jax.ShapeDtypeStruct((t, d), F32),
        compiler_params=pltpu.CompilerParams(dimension_semantics=("parallel",),
                                             vmem_limit_bytes=VMEM_LIMIT),
        name="ffn",
    )(x, norm_g, w_gate, w_up, w_down, final_g)


def _rope_tables(positions):
    half = ROPE_DIM // 2
    inv = ROPE_THETA ** (-jnp.arange(0, ROPE_DIM, 2, dtype=F32) / ROPE_DIM)
    ang = positions.astype(F32)[..., None] * inv
    cos, sin = jnp.cos(ang), jnp.sin(ang)
    rest = positions.shape + (HEAD_DIM - 2 * half,)
    cos_h = jnp.concatenate([cos, cos, jnp.ones(rest, F32)], axis=-1)
    sin_h = jnp.concatenate([-sin, sin, jnp.zeros(rest, F32)], axis=-1)
    reps = LANES // HEAD_DIM
    return jnp.tile(cos_h, (1, 1, reps)), jnp.tile(sin_h, (1, 1, reps))


def _mask_tables(s):
    n_cmp = s // CMP_STRIDE
    ks = jnp.arange(n_cmp)[:, None] * CMP_STRIDE
    bs = jnp.arange(LANES)[None, :] * SEL_BLOCK
    overlap = jnp.clip(jnp.minimum(ks + CMP_BLOCK, bs + SEL_BLOCK) - jnp.maximum(ks, bs), 0)
    overlap = (jnp.where(bs < s, overlap, 0).astype(F32) / CMP_STRIDE).astype(BF16)
    masked = lambda visible: jnp.where(visible, 0.0, -MASK_BIG).astype(BF16)
    neg_e = -(jnp.arange(s)[None, :] // SEL_BLOCK == jnp.arange(LANES)[:, None]).astype(BF16)
    tok = jnp.arange(s).reshape(s // TQ, TQ, 1)
    cmask = masked(jnp.arange(n_cmp)[None, None, :] * CMP_STRIDE + (CMP_BLOCK - 1) <= tok)
    tau = jnp.arange(TQ)[:, None]
    col = jnp.arange(WIN_KEYS + WINDOW)[None, :]
    band = masked((col > tau) & (col <= tau + WINDOW))
    tri = masked(jnp.arange(TQ)[None, :] <= tau)
    return overlap, neg_e, cmask, band, tri


def kernel(x, positions, norm_mix_g, w_in, cmp_k_pos, cmp_k_w1, cmp_k_w2, cmp_v_pos, cmp_v_w1, cmp_v_w2,
           conv_w, conv_b, conv_norm_g, conv_norm_b, w_conv_out, b_conv_out, w_out,
           norm_ffn_g, w_ffn_gate, w_ffn_up, w_ffn_down, norm_final_g):
    b, s, d = x.shape
    assert d == D_MODEL and s % KT == 0 and s // SEL_BLOCK <= LANES and s >= WIN_KEYS
    row_vec = lambda v: v.reshape(1, -1).astype(F32)

    o_kv = N_HEADS * HEAD_DIM
    o_gn = o_kv + 6 * KV_COLS
    o_ua = o_gn + 3 * N_HEADS
    wq = w_in[:, :o_kv].astype(BF16)
    wkv = w_in[:, o_kv:o_gn].astype(BF16)
    wgn = w_in[:, o_gn:o_ua].reshape(d, 3, N_KV, GROUP).transpose(0, 2, 1, 3).reshape(d, N_KV, 3 * GROUP)
    wgn = jnp.pad(wgn, ((0, 0), (0, 0), (0, LANES - 3 * GROUP))).reshape(d, N_KV * LANES).astype(BF16)
    wua, wub, wga, wgc = (w_in[:, o_ua + k * d:o_ua + (k + 1) * d].astype(BF16) for k in range(4))

    cosf, sinf = _rope_tables(positions)
    q, kc, vc, kst, vs, kwt, vw, gates, z, ga, gc = _proj(
        x, row_vec(norm_mix_g), cosf, sinf, wq, wkv, wgn, wua, wub, wga, wgc, tm=512)

    n_chunks = s // CMP_STRIDE
    chunk_w = CMP_STRIDE * HEAD_DIM
    kct = _compress(kc.reshape(b, N_KV, n_chunks, chunk_w), cmp_k_pos.reshape(1, -1),
                    cmp_k_w1.astype(BF16), cmp_k_w2.astype(BF16), True)
    vcp = _compress(vc.reshape(b, N_KV, n_chunks, chunk_w), cmp_v_pos.reshape(1, -1),
                    cmp_v_w1.astype(BF16), cmp_v_w2.astype(BF16), False)

    attn = _nsa(q, kct, vcp, kst, vs, kwt, vw, gates, *_mask_tables(s))

    x1 = _mix(x, z, attn, ga, gc, conv_w.reshape(CONV_WIDTH, d), row_vec(conv_b), row_vec(conv_norm_g),
              row_vec(conv_norm_b), w_conv_out.astype(BF16), row_vec(b_conv_out), w_out.astype(BF16), tm=256)

    out = _ffn(x1.reshape(b * s, d), row_vec(norm_ffn_g), w_ffn_gate.astype(BF16), w_ffn_up.astype(BF16),
               w_ffn_down.astype(BF16), row_vec(norm_final_g), tm=512, chunk=D_FF // 2)
    return out.reshape(b, s, d)
```

```python
import functools
import math

import jax
import jax.numpy as jnp
from jax import lax
from jax.experimental import pallas as pl
from jax.experimental.pallas import tpu as pltpu

F32 = jnp.float32
BF16 = jnp.bfloat16

D_MODEL = 1024
N_HEADS = 16
HEAD_DIM = 64
N_KV = 4
GROUP = N_HEADS // N_KV
ROPE_DIM = HEAD_DIM // 4
ROPE_THETA = 500000.0
CMP_BLOCK = 32
CMP_STRIDE = 16
CMP_HIDDEN = 4 * HEAD_DIM
SEL_BLOCK = 64
SEL_TOPK = 16
WINDOW = 512
CONV_WIDTH = 31
D_FF = ((8 * D_MODEL // 3 + 255) // 256) * 256
EPS = 1e-6

LANES = 128
SUBLANES = 8
KV_COLS = N_KV * HEAD_DIM
TQ = LANES
QROWS = GROUP * TQ
QT_PER_STEP = 2
STEP_TOKENS = QT_PER_STEP * TQ
STEP_ROWS = QT_PER_STEP * QROWS
KT = 512
SOFTMAX_ROWS = 64
WIN_KEYS = WINDOW + TQ
N_FORCED = 3
MASK_BIG = 2.0 ** 100
Q_SCALE = HEAD_DIM ** -0.5 * math.log2(math.e)
CONV_HALO = 32
VMEM_LIMIT = 56 * 1024 * 1024


def _sigmoid(v):
    return 1.0 / (1.0 + jnp.exp(-v))


def _const_spec(shape):
    nd = len(shape)
    return pl.BlockSpec(shape, lambda *_: (0,) * nd, pipeline_mode=pl.Buffered(1))


def _proj_kernel(x_ref, g_ref, cos_ref, sin_ref, wq_ref, wkv_ref, wgn_ref, wua_ref, wub_ref, wga_ref,
                 wgc_ref, q_ref, kc_ref, vc_ref, kst_ref, vs_ref, kwt_ref, vw_ref, gates_ref, z_ref,
                 ga_ref, gc_ref):
    x = x_ref[0]
    h = (x * lax.rsqrt(jnp.mean(x * x, axis=-1, keepdims=True) + EPS) * g_ref[...]).astype(BF16)
    cosf = cos_ref[0]
    sinf = sin_ref[0]

    def rope(y):
        w = y.shape[1]
        lane = lax.broadcasted_iota(jnp.int32, y.shape, 1)
        partner = jnp.where((lane & (HEAD_DIM - 1)) < ROPE_DIM // 2,
                            pltpu.roll(y, w - ROPE_DIM // 2, 1), pltpu.roll(y, ROPE_DIM // 2, 1))
        reps = w // LANES
        return (y * jnp.concatenate([cosf] * reps, axis=1) + partner * jnp.concatenate([sinf] * reps, axis=1))

    def one_head(v_pair, odd):
        lane = lax.broadcasted_iota(jnp.int32, v_pair.shape, 1)
        v = pltpu.roll(v_pair, HEAD_DIM, 1) if odd else v_pair
        return jnp.where(lane < HEAD_DIM, v, 0.0).astype(BF16)

    yq = jnp.dot(h, wq_ref[...], preferred_element_type=F32)
    q_ref[0] = (rope(yq) * Q_SCALE).astype(BF16)

    ykv = jnp.dot(h, wkv_ref[...], preferred_element_type=F32)
    kc = rope(ykv[:, 0 * KV_COLS:1 * KV_COLS])
    vc = ykv[:, 1 * KV_COLS:2 * KV_COLS]
    for g in range(N_KV):
        kc_ref[0, g] = kc[:, g * HEAD_DIM:(g + 1) * HEAD_DIM]
        vc_ref[0, g] = vc[:, g * HEAD_DIM:(g + 1) * HEAD_DIM]
    kst_ref[0] = rope(ykv[:, 2 * KV_COLS:3 * KV_COLS]).T.astype(BF16)
    kwt_ref[0] = rope(ykv[:, 4 * KV_COLS:5 * KV_COLS]).T.astype(BF16)
    vs = ykv[:, 3 * KV_COLS:4 * KV_COLS]
    vw = ykv[:, 5 * KV_COLS:6 * KV_COLS]
    for g in range(N_KV):
        pair = slice((g // 2) * LANES, (g // 2 + 1) * LANES)
        vs_ref[0, g] = jnp.concatenate([one_head(vs[:, pair], g % 2 == 1),
                                        jnp.ones((vs.shape[0], LANES), BF16)], axis=1)
        vw_ref[0, g] = one_head(vw[:, pair], g % 2 == 1)

    gates = _sigmoid(jnp.dot(h, wgn_ref[...], preferred_element_type=F32))
    for g in range(N_KV):
        gates_ref[0, g] = gates[:, g * LANES:(g + 1) * LANES]

    ua = jnp.dot(h, wua_ref[...], preferred_element_type=F32)
    ub = jnp.dot(h, wub_ref[...], preferred_element_type=F32)
    z_ref[0] = ua * _sigmoid(ub)
    ga_ref[0] = _sigmoid(jnp.dot(h, wga_ref[...], preferred_element_type=F32))
    gc_ref[0] = _sigmoid(jnp.dot(h, wgc_ref[...], preferred_element_type=F32))


def _proj(x, norm_g, cosf, sinf, wq, wkv, wgn, wua, wub, wga, wgc, tm):
    b, s, d = x.shape
    grid = (b, s // tm)
    row = lambda w: pl.BlockSpec((1, tm, w), lambda bi, i: (bi, i, 0))
    per_group = lambda w: pl.BlockSpec((1, N_KV, tm, w), lambda bi, i: (bi, 0, i, 0))
    transposed = pl.BlockSpec((1, KV_COLS, tm), lambda bi, i: (bi, 0, i))
    out_shape = (
        jax.ShapeDtypeStruct((b, s, d), BF16),
        jax.ShapeDtypeStruct((b, N_KV, s, HEAD_DIM), F32),
        jax.ShapeDtypeStruct((b, N_KV, s, HEAD_DIM), F32),
        jax.ShapeDtypeStruct((b, KV_COLS, s), BF16),
        jax.ShapeDtypeStruct((b, N_KV, s, 2 * LANES), BF16),
        jax.ShapeDtypeStruct((b, KV_COLS, s), BF16),
        jax.ShapeDtypeStruct((b, N_KV, s, LANES), BF16),
        jax.ShapeDtypeStruct((b, N_KV, s, LANES), F32),
        jax.ShapeDtypeStruct((b, s, d), F32),
        jax.ShapeDtypeStruct((b, s, d), F32),
        jax.ShapeDtypeStruct((b, s, d), F32),
    )
    out_specs = (row(d), per_group(HEAD_DIM), per_group(HEAD_DIM), transposed, per_group(2 * LANES),
                 transposed, per_group(LANES), per_group(LANES), row(d), row(d), row(d))
    in_specs = [row(d), _const_spec((1, d)), row(LANES), row(LANES),
                _const_spec(wq.shape), _const_spec(wkv.shape), _const_spec(wgn.shape),
                _const_spec(wua.shape), _const_spec(wub.shape), _const_spec(wga.shape),
                _const_spec(wgc.shape)]
    return pl.pallas_call(
        _proj_kernel, grid=grid, in_specs=in_specs, out_specs=out_specs, out_shape=out_shape,
        compiler_params=pltpu.CompilerParams(dimension_semantics=("parallel", "parallel"),
                                             vmem_limit_bytes=VMEM_LIMIT),
        name="proj",
    )(x, norm_g, cosf, sinf, wq, wkv, wgn, wua, wub, wga, wgc)


def _compress_kernel(c_ref, pos_ref, w1_ref, w2_ref, o_ref, *, transpose_out):
    half = CMP_STRIDE * HEAD_DIM
    c = c_ref[0, 0]
    top = (c + pos_ref[:, :half]).astype(BF16)
    bot = (c + pos_ref[:, half:]).astype(BF16)
    a = jnp.dot(top, w1_ref[:half, :], preferred_element_type=F32)
    bm = jnp.dot(bot, w1_ref[half:, :], preferred_element_type=F32)
    n = c.shape[0]
    hid = a + pltpu.roll(bm, n - 1, 0)
    hid = hid * _sigmoid(hid)
    out = jnp.dot(hid.astype(BF16), w2_ref[...], preferred_element_type=F32)
    if transpose_out:
        o_ref[0, 0] = out.T.astype(BF16)
    else:
        lane = lax.broadcasted_iota(jnp.int32, (n, HEAD_DIM), 1)
        ones_col = jnp.where(lane == 0, 1.0, 0.0)
        o_ref[0, 0] = jnp.concatenate([out, ones_col], axis=1).astype(BF16)


def _compress(chunks, pos_flat, w1, w2, transpose_out):
    b, g, n, width = chunks.shape
    if transpose_out:
        out_shape = jax.ShapeDtypeStruct((b, g, HEAD_DIM, n), BF16)
        out_spec = pl.BlockSpec((1, 1, HEAD_DIM, n), lambda bi, gi: (bi, gi, 0, 0))
    else:
        out_shape = jax.ShapeDtypeStruct((b, g, n, LANES), BF16)
        out_spec = pl.BlockSpec((1, 1, n, LANES), lambda bi, gi: (bi, gi, 0, 0))
    return pl.pallas_call(
        functools.partial(_compress_kernel, transpose_out=transpose_out),
        grid=(b, g),
        in_specs=[pl.BlockSpec((1, 1, n, width), lambda bi, gi: (bi, gi, 0, 0)),
                  _const_spec(pos_flat.shape), _const_spec(w1.shape), _const_spec(w2.shape)],
        out_specs=out_spec, out_shape=out_shape,
        compiler_params=pltpu.CompilerParams(dimension_semantics=("parallel", "parallel"),
                                             vmem_limit_bytes=VMEM_LIMIT),
        name="compress_k" if transpose_out else "compress_v",
    )(chunks, pos_flat, w1, w2)


def _nsa_front(step_t0, q_ref, kct_ref, vc_ref, kst_ref, vs_ref, kwt_ref, vw_ref, gates_ref, ov_ref,
               cmask_ref, band_ref, tri_ref, qi_ref, qs_ref, m_ref, acc_ref, part_ref, gsel_ref):
    tiles = range(QT_PER_STEP)
    t0 = [pl.multiple_of(step_t0 + h * TQ, TQ) for h in tiles]
    own = [slice(h * QROWS, (h + 1) * QROWS) for h in tiles]
    head_rows = lambda h, r: slice(h * QROWS + r * TQ, h * QROWS + (r + 1) * TQ)

    eye = jnp.where(lax.broadcasted_iota(jnp.int32, (TQ, LANES), 0)
                    == lax.broadcasted_iota(jnp.int32, (TQ, LANES), 1), 1.0, 0.0).astype(BF16)
    for h in tiles:
        qb = q_ref[0, h * TQ:(h + 1) * TQ, :]
        for r in range(GROUP):
            q_r = qb[:, r * HEAD_DIM:(r + 1) * HEAD_DIM]
            qi_ref[head_rows(h, r), 0:LANES] = eye
            qi_ref[head_rows(h, r), LANES:LANES + HEAD_DIM] = q_r
            qs_ref[head_rows(h, r), LANES:LANES + HEAD_DIM] = q_r
    qi = [qi_ref[own[h], :] for h in tiles]

    w0 = [pl.multiple_of(jnp.maximum(t - WINDOW, 0), LANES) for t in t0]
    shift = [pl.multiple_of(jnp.maximum(WINDOW - t, 0), LANES) for t in t0]
    s_cmp = [jnp.dot(qi[h], jnp.concatenate([cmask_ref[h], kct_ref[0, 0]], axis=0),
                     preferred_element_type=F32) for h in tiles]
    s_win = [jnp.dot(qi[h], jnp.concatenate([band_ref[:, pl.ds(shift[h], WIN_KEYS)],
                                             kwt_ref[0, :, pl.ds(w0[h], WIN_KEYS)]], axis=0),
                     preferred_element_type=F32) for h in tiles]
    s_dia = [jnp.dot(qi[h], jnp.concatenate([tri_ref[...], kst_ref[0, :, pl.ds(t0[h], TQ)]], axis=0),
                     preferred_element_type=F32) for h in tiles]

    o_cmp, imp_t = [], []
    for h in tiles:
        m = jnp.max(s_cmp[h], axis=1, keepdims=True)
        e = jnp.exp2(s_cmp[h] - m)
        d = jnp.sum(e, axis=1, keepdims=True)
        dinv = jnp.where(m > -0.5 * MASK_BIG, 1.0 / d, 0.0)
        p = e * dinv
        o_cmp.append(jnp.dot(e.astype(BF16), vc_ref[0, 0], preferred_element_type=F32) * dinv)
        psum = p[0:TQ] + p[TQ:2 * TQ] + p[2 * TQ:3 * TQ] + p[3 * TQ:4 * TQ]
        p_hi = psum.astype(BF16)
        p_lo = (psum - p_hi.astype(F32)).astype(BF16)
        imp = (jnp.dot(p_hi, ov_ref[...], preferred_element_type=F32)
               + jnp.dot(p_lo, ov_ref[...], preferred_element_type=F32))
        imp_t.append(imp.T)

    o_win = []
    for h in tiles:
        ew = jnp.exp2(s_win[h] - jnp.max(s_win[h], axis=1, keepdims=True))
        o = jnp.dot(ew.astype(BF16), vw_ref[0, 0, pl.ds(w0[h], WIN_KEYS), :], preferred_element_type=F32)
        o_win.append(o * (1.0 / jnp.sum(ew, axis=1, keepdims=True)))
    for h in tiles:
        m0 = jnp.max(s_dia[h], axis=1, keepdims=True)
        pd = jnp.exp2(s_dia[h] - m0)
        acc_ref[own[h], :] = jnp.dot(pd.astype(BF16), vs_ref[0, 0, pl.ds(t0[h], TQ), :],
                                     preferred_element_type=F32)
        m_ref[own[h], :] = jnp.broadcast_to(m0, (QROWS, LANES))

    for h in tiles:
        gt = gates_ref[0, 0, h * TQ:(h + 1) * TQ, :]
        for r in range(GROUP):
            rows = slice(r * TQ, (r + 1) * TQ)
            part_ref[head_rows(h, r), :] = (gt[:, r:r + 1] * o_cmp[h][rows, :]
                                            + gt[:, 2 * GROUP + r:2 * GROUP + r + 1] * o_win[h][rows, :])
            gsel_ref[head_rows(h, r), :] = jnp.broadcast_to(gt[:, GROUP + r:GROUP + r + 1], (TQ, LANES))

    j_i = lax.broadcasted_iota(jnp.int32, (LANES, TQ), 0)
    j_f = j_i.astype(F32)
    tok = lax.broadcasted_iota(jnp.int32, (LANES, TQ), 1)
    work, taken = [], []
    for h in tiles:
        qblk = (t0[h] + tok) // SEL_BLOCK
        taken.append(jnp.where(j_i == 0, 1.0, jnp.where(j_i == qblk - 1, 1.0, 0.0)))
        work.append(jnp.where(j_i >= 1, jnp.where(j_i <= qblk - 2, imp_t[h], -1.0), -1.0))
    for _ in range(SEL_TOPK - N_FORCED):
        for h in tiles:
            mx = jnp.max(work[h], axis=0, keepdims=True)
            idx = jnp.min(jnp.where(work[h] == mx, j_f, 1e9), axis=0, keepdims=True)
            pick = j_f == jnp.where(mx >= 0.0, idx, -1.0)
            work[h] = jnp.where(pick, -1.0, work[h])
            taken[h] = jnp.where(pick, 1.0, taken[h])
    for h in tiles:
        not_sel = jnp.where(j_i * SEL_BLOCK < t0[h], jnp.where(taken[h] > 0.0, 0.0, MASK_BIG), MASK_BIG)
        not_sel = not_sel.T.astype(BF16)
        for r in range(GROUP):
            qs_ref[head_rows(h, r), 0:LANES] = not_sel


def _nsa_kernel(q_ref, kct_ref, vc_ref, kst_ref, vs_ref, kwt_ref, vw_ref, gates_ref, ov_ref, nege_ref,
                cmask_ref, band_ref, tri_ref, o_ref, qi_ref, qs_ref, m_ref, acc_ref, part_ref, gsel_ref,
                s0_ref, s1_ref, p0_ref, p1_ref, alpha0_ref, alpha1_ref):
    step_t0 = pl.multiple_of(pl.program_id(2) * STEP_TOKENS, STEP_TOKENS)
    _nsa_front(step_t0, q_ref, kct_ref, vc_ref, kst_ref, vs_ref, kwt_ref, vw_ref, gates_ref, ov_ref, cmask_ref,
               band_ref, tri_ref, qi_ref, qs_ref, m_ref, acc_ref, part_ref, gsel_ref)

    n_tiles = (step_t0 + (QT_PER_STEP - 1) * TQ + KT - 1) // KT
    last_tile = kst_ref.shape[-1] // KT - 1

    def produce(s_ref, c):
        k0 = pl.multiple_of(jnp.minimum(c, last_tile) * KT, KT)
        kaug = jnp.concatenate([nege_ref[:, pl.ds(k0, KT)], kst_ref[0, :, pl.ds(k0, KT)]], axis=0)
        s_ref[...] = jnp.dot(qs_ref[...], kaug, preferred_element_type=F32)

    def consume(buf, c, backwards):
        s_ref, p_ref, alpha_ref = buf
        k0 = pl.multiple_of(c * KT, KT)
        chunks = range(STEP_ROWS // SOFTMAX_ROWS)
        for chunk in (reversed(chunks) if backwards else chunks):
            rows = slice(chunk * SOFTMAX_ROWS, (chunk + 1) * SOFTMAX_ROWS)
            sc = s_ref[rows, :]
            m_old = m_ref[rows, :]
            m_new = jnp.maximum(m_old, jnp.max(sc, axis=1, keepdims=True))
            p_ref[rows, :] = jnp.exp2(sc - jnp.concatenate([m_new] * (KT // LANES), axis=1)).astype(BF16)
            alpha_ref[rows, :] = jnp.exp2(m_old - m_new)
            m_ref[rows, :] = m_new
        alpha = jnp.concatenate([alpha_ref[...]] * 2, axis=1)
        acc_ref[...] = (alpha * acc_ref[...]
                        + jnp.dot(p_ref[...], vs_ref[0, 0, pl.ds(k0, KT), :], preferred_element_type=F32))

    buf0, buf1 = (s0_ref, p0_ref, alpha0_ref), (s1_ref, p1_ref, alpha1_ref)
    produce(s0_ref, 0)

    def tile_pair(pair, carry):
        c = 2 * pair
        produce(s1_ref, c + 1)
        consume(buf0, c, backwards=False)
        produce(s0_ref, c + 2)
        consume(buf1, c + 1, backwards=True)
        return carry

    lax.fori_loop(0, n_tiles // 2, tile_pair, 0)

    @pl.when(n_tiles % 2 == 1)
    def _():
        consume(buf0, n_tiles - 1, backwards=False)

    o_all = part_ref[...] + gsel_ref[...] * acc_ref[:, 0:LANES] * (1.0 / acc_ref[:, LANES:2 * LANES])
    for h in range(QT_PER_STEP):
        for r in range(GROUP):
            src = slice(h * QROWS + r * TQ, h * QROWS + (r + 1) * TQ)
            o_ref[0, h * TQ:(h + 1) * TQ, r * HEAD_DIM:(r + 1) * HEAD_DIM] = o_all[src, :HEAD_DIM]


def _nsa(q, kct, vc, kst, vs, kwt, vw, gates, ov, nege, cmask, band, tri):
    b, s, d = q.shape
    n_cmp = kct.shape[-1]
    grid = (b, N_KV, s // STEP_TOKENS)
    per_bg = lambda shape: pl.BlockSpec((1, 1) + shape, lambda bi, gi, i: (bi, gi, 0, 0))
    kt_spec = pl.BlockSpec((1, HEAD_DIM, s), lambda bi, gi, i: (bi, gi, 0))
    whole = lambda a: pl.BlockSpec(a.shape, lambda bi, gi, i: (0,) * a.ndim)
    in_specs = [
        pl.BlockSpec((1, STEP_TOKENS, GROUP * HEAD_DIM), lambda bi, gi, i: (bi, i, gi)),
        per_bg((HEAD_DIM, n_cmp)), per_bg((n_cmp, LANES)),
        kt_spec, per_bg((s, 2 * LANES)), kt_spec, per_bg((s, LANES)),
        pl.BlockSpec((1, 1, STEP_TOKENS, LANES), lambda bi, gi, i: (bi, gi, i, 0)),
        whole(ov), whole(nege),
        pl.BlockSpec((QT_PER_STEP, TQ, n_cmp), lambda bi, gi, i: (i, 0, 0)),
        whole(band), whole(tri),
    ]
    return pl.pallas_call(
        _nsa_kernel, grid=grid, in_specs=in_specs,
        out_specs=pl.BlockSpec((1, STEP_TOKENS, GROUP * HEAD_DIM), lambda bi, gi, i: (bi, i, gi)),
        out_shape=jax.ShapeDtypeStruct((b, s, d), F32),
        scratch_shapes=[pltpu.VMEM((STEP_ROWS, LANES + HEAD_DIM), BF16),
                        pltpu.VMEM((STEP_ROWS, LANES + HEAD_DIM), BF16),
                        pltpu.VMEM((STEP_ROWS, LANES), F32),
                        pltpu.VMEM((STEP_ROWS, 2 * LANES), F32),
                        pltpu.VMEM((STEP_ROWS, LANES), F32),
                        pltpu.VMEM((STEP_ROWS, LANES), F32),
                        pltpu.VMEM((STEP_ROWS, KT), F32),
                        pltpu.VMEM((STEP_ROWS, KT), F32),
                        pltpu.VMEM((STEP_ROWS, KT), BF16),
                        pltpu.VMEM((STEP_ROWS, KT), BF16),
                        pltpu.VMEM((STEP_ROWS, LANES), F32),
                        pltpu.VMEM((STEP_ROWS, LANES), F32)],
        compiler_params=pltpu.CompilerParams(dimension_semantics=("parallel", "parallel", "arbitrary"),
                                             vmem_limit_bytes=VMEM_LIMIT),
        name="nsa",
    )(q, kct, vc, kst, vs, kwt, vw, gates, ov, nege, cmask, band, tri)


def _mix_kernel(x_ref, z_ref, zprev_ref, attn_ref, ga_ref, gc_ref, cw_ref, cb_ref, lg_ref, lb_ref,
                wco_ref, bco_ref, wout_ref, o_ref, zbuf_ref, zshift_ref):
    tm = z_ref.shape[1]
    halo = zprev_ref[0]
    zbuf_ref[0:CONV_HALO, :] = jnp.where(pl.program_id(1) > 0, halo, jnp.zeros_like(halo))
    zbuf_ref[CONV_HALO:CONV_HALO + tm, :] = z_ref[0]
    first = CONV_HALO - (CONV_WIDTH - 1)
    zc = cb_ref[...]
    for phase in range(SUBLANES):
        taps = [k for k in range(CONV_WIDTH) if (first + k) % SUBLANES == phase]
        span = max(first + k - phase for k in taps) + tm
        zshift_ref[phase, 0:span, :] = zbuf_ref[phase:phase + span, :]
        for k in taps:
            off = first + k - phase
            zc = zc + cw_ref[k:k + 1, :] * zshift_ref[phase, off:off + tm, :]
    mu = jnp.mean(zc, axis=-1, keepdims=True)
    dev = zc - mu
    var = jnp.mean(dev * dev, axis=-1, keepdims=True)
    zn = dev * lax.rsqrt(var + EPS) * lg_ref[...] + lb_ref[...]
    act = (zn * _sigmoid(zn)).astype(BF16)
    conv = jnp.dot(act, wco_ref[...], preferred_element_type=F32) + bco_ref[...]
    merged = (ga_ref[0] * attn_ref[0] + gc_ref[0] * conv).astype(BF16)
    o_ref[0] = x_ref[0] + jnp.dot(merged, wout_ref[...], preferred_element_type=F32)


def _mix(x, z, attn, ga, gc, conv_w, conv_b, ln_g, ln_b, w_co, b_co, w_out, tm):
    b, s, d = x.shape
    row = pl.BlockSpec((1, tm, d), lambda bi, i: (bi, i, 0))
    ratio = tm // CONV_HALO
    prev = pl.BlockSpec((1, CONV_HALO, d), lambda bi, i: (bi, jnp.maximum(i * ratio - 1, 0), 0))
    in_specs = [row, row, prev, row, row, row, _const_spec(conv_w.shape), _const_spec((1, d)),
                _const_spec((1, d)), _const_spec((1, d)), _const_spec(w_co.shape), _const_spec((1, d)),
                _const_spec(w_out.shape)]
    return pl.pallas_call(
        _mix_kernel, grid=(b, s // tm), in_specs=in_specs, out_specs=row,
        out_shape=jax.ShapeDtypeStruct((b, s, d), F32),
        scratch_shapes=[pltpu.VMEM((CONV_HALO + tm, d), F32),
                        pltpu.VMEM((SUBLANES, CONV_HALO + tm, d), F32)],
        compiler_params=pltpu.CompilerParams(dimension_semantics=("parallel", "parallel"),
                                             vmem_limit_bytes=VMEM_LIMIT),
        name="mix",
    )(x, z, z, attn, ga, gc, conv_w, conv_b, ln_g, ln_b, w_co, b_co, w_out)


def _ffn_kernel(x_ref, g_ref, wg_ref, wu_ref, wd_ref, gf_ref, o_ref, *, chunk):
    x = x_ref[...]
    h = (x * lax.rsqrt(jnp.mean(x * x, axis=-1, keepdims=True) + EPS) * g_ref[...]).astype(BF16)
    y = x
    for c in range(D_FF // chunk):
        cols = slice(c * chunk, (c + 1) * chunk)
        gate = jnp.dot(h, wg_ref[:, cols], preferred_element_type=F32)
        up = jnp.dot(h, wu_ref[:, cols], preferred_element_type=F32)
        act = (gate * _sigmoid(gate) * up).astype(BF16)
        y = y + jnp.dot(act, wd_ref[cols, :], preferred_element_type=F32)
    o_ref[...] = y * lax.rsqrt(jnp.mean(y * y, axis=-1, keepdims=True) + EPS) * gf_ref[...]


def _ffn(x, norm_g, w_gate, w_up, w_down, final_g, tm, chunk):
    t, d = x.shape
    row = pl.BlockSpec((tm, d), lambda i: (i, 0))
    in_specs = [row, _const_spec((1, d)), _const_spec(w_gate.shape), _const_spec(w_up.shape),
                _const_spec(w_down.shape), _const_spec((1, d))]
    return pl.pallas_call(
        functools.partial(_ffn_kernel, chunk=chunk), grid=(t // tm,), in_specs=in_specs, out_specs=row,
        out_shape=jax.ShapeDtypeStruct((t, d), F32),
        compiler_params=pltpu.CompilerParams(dimension_semantics=("parallel",),
                                             vmem_limit_bytes=VMEM_LIMIT),
        name="ffn",
    )(x, norm_g, w_gate, w_up, w_down, final_g)


def _rope_tables(positions):
    half = ROPE_DIM // 2
    inv = ROPE_THETA ** (-jnp.arange(0, ROPE_DIM, 2, dtype=F32) / ROPE_DIM)
    ang = positions.astype(F32)[..., None] * inv
    cos, sin = jnp.cos(ang), jnp.sin(ang)
    rest = positions.shape + (HEAD_DIM - 2 * half,)
    cos_h = jnp.concatenate([cos, cos, jnp.ones(rest, F32)], axis=-1)
    sin_h = jnp.concatenate([-sin, sin, jnp.zeros(rest, F32)], axis=-1)
    reps = LANES // HEAD_DIM
    return jnp.tile(cos_h, (1, 1, reps)), jnp.tile(sin_h, (1, 1, reps))


def _mask_tables(s):
    n_cmp = s // CMP_STRIDE
    ks = jnp.arange(n_cmp)[:, None] * CMP_STRIDE
    bs = jnp.arange(LANES)[None, :] * SEL_BLOCK
    overlap = jnp.clip(jnp.minimum(ks + CMP_BLOCK, bs + SEL_BLOCK) - jnp.maximum(ks, bs), 0)
    overlap = (jnp.where(bs < s, overlap, 0).astype(F32) / CMP_STRIDE).astype(BF16)
    masked = lambda visible: jnp.where(visible, 0.0, -MASK_BIG).astype(BF16)
    neg_e = -(jnp.arange(s)[None, :] // SEL_BLOCK == jnp.arange(LANES)[:, None]).astype(BF16)
    tok = jnp.arange(s).reshape(s // TQ, TQ, 1)
    cmask = masked(jnp.arange(n_cmp)[None, None, :] * CMP_STRIDE + (CMP_BLOCK - 1) <= tok)
    tau = jnp.arange(TQ)[:, None]
    col = jnp.arange(WIN_KEYS + WINDOW)[None, :]
    band = masked((col > tau) & (col <= tau + WINDOW))
    tri = masked(jnp.arange(TQ)[None, :] <= tau)
    return overlap, neg_e, cmask, band, tri


def kernel(x, positions, norm_mix_g, w_in, cmp_k_pos, cmp_k_w1, cmp_k_w2, cmp_v_pos, cmp_v_w1, cmp_v_w2,
           conv_w, conv_b, conv_norm_g, conv_norm_b, w_conv_out, b_conv_out, w_out,
           norm_ffn_g, w_ffn_gate, w_ffn_up, w_ffn_down, norm_final_g):
    b, s, d = x.shape
    assert d == D_MODEL and s % KT == 0 and s // SEL_BLOCK <= LANES and s >= WIN_KEYS
    row_vec = lambda v: v.reshape(1, -1).astype(F32)

    o_kv = N_HEADS * HEAD_DIM
    o_gn = o_kv + 6 * KV_COLS
    o_ua = o_gn + 3 * N_HEADS
    wq = w_in[:, :o_kv].astype(BF16)
    wkv = w_in[:, o_kv:o_gn].astype(BF16)
    wgn = w_in[:, o_gn:o_ua].reshape(d, 3, N_KV, GROUP).transpose(0, 2, 1, 3).reshape(d, N_KV, 3 * GROUP)
    wgn = jnp.pad(wgn, ((0, 0), (0, 0), (0, LANES - 3 * GROUP))).reshape(d, N_KV * LANES).astype(BF16)
    wua, wub, wga, wgc = (w_in[:, o_ua + k * d:o_ua + (k + 1) * d].astype(BF16) for k in range(4))

    cosf, sinf = _rope_tables(positions)
    q, kc, vc, kst, vs, kwt, vw, gates, z, ga, gc = _proj(
        x, row_vec(norm_mix_g), cosf, sinf, wq, wkv, wgn, wua, wub, wga, wgc, tm=512)

    n_chunks = s // CMP_STRIDE
    chunk_w = CMP_STRIDE * HEAD_DIM
    kct = _compress(kc.reshape(b, N_KV, n_chunks, chunk_w), cmp_k_pos.reshape(1, -1),
                    cmp_k_w1.astype(BF16), cmp_k_w2.astype(BF16), True)
    vcp = _compress(vc.reshape(b, N_KV, n_chunks, chunk_w), cmp_v_pos.reshape(1, -1),
                    cmp_v_w1.astype(BF16), cmp_v_w2.astype(BF16), False)

    attn = _nsa(q, kct, vcp, kst, vs, kwt, vw, gates, *_mask_tables(s))

    x1 = _mix(x, z, attn, ga, gc, conv_w.reshape(CONV_WIDTH, d), row_vec(conv_b), row_vec(conv_norm_g),
              row_vec(conv_norm_b), w_conv_out.astype(BF16), row_vec(b_conv_out), w_out.astype(BF16), tm=256)

    out = _ffn(x1.reshape(b * s, d), row_vec(norm_ffn_g), w_ffn_gate.astype(BF16), w_ffn_up.astype(BF16),
               w_ffn_down.astype(BF16), row_vec(norm_final_g), tm=512, chunk=D_FF // 2)
    return out.reshape(b, s, d)
```

```python
import functools
import math

import jax
import jax.numpy as jnp
from jax import lax
from jax.experimental import pallas as pl
from jax.experimental.pallas import tpu as pltpu

F32 = jnp.float32
BF16 = jnp.bfloat16

D_MODEL = 1024
N_HEADS = 16
HEAD_DIM = 64
N_KV = 4
GROUP = N_HEADS // N_KV
ROPE_DIM = HEAD_DIM // 4
ROPE_THETA = 500000.0
CMP_BLOCK = 32
CMP_STRIDE = 16
CMP_HIDDEN = 4 * HEAD_DIM
SEL_BLOCK = 64
SEL_TOPK = 16
WINDOW = 512
CONV_WIDTH = 31
D_FF = ((8 * D_MODEL // 3 + 255) // 256) * 256
EPS = 1e-6

LANES = 128
SUBLANES = 8
KV_COLS = N_KV * HEAD_DIM
TQ = LANES
QROWS = GROUP * TQ
QT_PER_STEP = 2
STEP_TOKENS = QT_PER_STEP * TQ
STEP_ROWS = QT_PER_STEP * QROWS
KT = 512
PACK_ROWS = 16
SOFTMAX_ROWS = 64
WIN_KEYS = WINDOW + TQ
N_FORCED = 3
MASK_BIG = 2.0 ** 100
Q_SCALE = HEAD_DIM ** -0.5 * math.log2(math.e)
CONV_HALO = 32
VMEM_LIMIT = 56 * 1024 * 1024


def _sigmoid(v):
    return 1.0 / (1.0 + jnp.exp(-v))


def _const_spec(shape):
    nd = len(shape)
    return pl.BlockSpec(shape, lambda *_: (0,) * nd, pipeline_mode=pl.Buffered(1))


def _proj_kernel(x_ref, g_ref, cos_ref, sin_ref, wq_ref, wkv_ref, wgn_ref, wua_ref, wub_ref, wga_ref,
                 wgc_ref, q_ref, kc_ref, vc_ref, kst_ref, vs_ref, kwt_ref, vw_ref, gates_ref, z_ref,
                 ga_ref, gc_ref):
    x = x_ref[0]
    h = (x * lax.rsqrt(jnp.mean(x * x, axis=-1, keepdims=True) + EPS) * g_ref[...]).astype(BF16)
    cosf = cos_ref[0]
    sinf = sin_ref[0]

    def rope(y):
        w = y.shape[1]
        lane = lax.broadcasted_iota(jnp.int32, y.shape, 1)
        partner = jnp.where((lane & (HEAD_DIM - 1)) < ROPE_DIM // 2,
                            pltpu.roll(y, w - ROPE_DIM // 2, 1), pltpu.roll(y, ROPE_DIM // 2, 1))
        reps = w // LANES
        return (y * jnp.concatenate([cosf] * reps, axis=1) + partner * jnp.concatenate([sinf] * reps, axis=1))

    def one_head(v_pair, odd):
        lane = lax.broadcasted_iota(jnp.int32, v_pair.shape, 1)
        v = pltpu.roll(v_pair, HEAD_DIM, 1) if odd else v_pair
        return jnp.where(lane < HEAD_DIM, v, 0.0).astype(BF16)

    yq = jnp.dot(h, wq_ref[...], preferred_element_type=F32)
    q_ref[0] = (rope(yq) * Q_SCALE).astype(BF16)

    ykv = jnp.dot(h, wkv_ref[...], preferred_element_type=F32)
    kc = rope(ykv[:, 0 * KV_COLS:1 * KV_COLS])
    vc = ykv[:, 1 * KV_COLS:2 * KV_COLS]
    for g in range(N_KV):
        kc_ref[0, g] = kc[:, g * HEAD_DIM:(g + 1) * HEAD_DIM].astype(BF16)
        vc_ref[0, g] = vc[:, g * HEAD_DIM:(g + 1) * HEAD_DIM].astype(BF16)
    kst_ref[0] = rope(ykv[:, 2 * KV_COLS:3 * KV_COLS]).T.astype(BF16)
    kwt_ref[0] = rope(ykv[:, 4 * KV_COLS:5 * KV_COLS]).T.astype(BF16)
    vs = ykv[:, 3 * KV_COLS:4 * KV_COLS]
    vw = ykv[:, 5 * KV_COLS:6 * KV_COLS]
    for g in range(N_KV):
        pair = slice((g // 2) * LANES, (g // 2 + 1) * LANES)
        ones = jnp.ones((vs.shape[0], LANES), BF16)
        vs_ref[0, g] = jnp.concatenate([one_head(vs[:, pair], g % 2 == 1), ones], axis=1)
        vw_ref[0, g] = jnp.concatenate([one_head(vw[:, pair], g % 2 == 1), ones], axis=1)

    gates = _sigmoid(jnp.dot(h, wgn_ref[...], preferred_element_type=F32))
    for g in range(N_KV):
        gates_ref[0, g] = gates[:, g * LANES:(g + 1) * LANES]

    ua = jnp.dot(h, wua_ref[...], preferred_element_type=F32)
    ub = jnp.dot(h, wub_ref[...], preferred_element_type=F32)
    z_ref[0] = ua * _sigmoid(ub)
    ga_ref[0] = _sigmoid(jnp.dot(h, wga_ref[...], preferred_element_type=F32))
    gc_ref[0] = _sigmoid(jnp.dot(h, wgc_ref[...], preferred_element_type=F32))


def _proj(x, norm_g, cosf, sinf, wq, wkv, wgn, wua, wub, wga, wgc, tm):
    b, s, d = x.shape
    grid = (b, s // tm)
    row = lambda w: pl.BlockSpec((1, tm, w), lambda bi, i: (bi, i, 0))
    per_group = lambda w: pl.BlockSpec((1, N_KV, tm, w), lambda bi, i: (bi, 0, i, 0))
    transposed = pl.BlockSpec((1, KV_COLS, tm), lambda bi, i: (bi, 0, i))
    out_shape = (
        jax.ShapeDtypeStruct((b, s, d), BF16),
        jax.ShapeDtypeStruct((b, N_KV, s, HEAD_DIM), BF16),
        jax.ShapeDtypeStruct((b, N_KV, s, HEAD_DIM), BF16),
        jax.ShapeDtypeStruct((b, KV_COLS, s), BF16),
        jax.ShapeDtypeStruct((b, N_KV, s, 2 * LANES), BF16),
        jax.ShapeDtypeStruct((b, KV_COLS, s), BF16),
        jax.ShapeDtypeStruct((b, N_KV, s, 2 * LANES), BF16),
        jax.ShapeDtypeStruct((b, N_KV, s, LANES), F32),
        jax.ShapeDtypeStruct((b, s, d), F32),
        jax.ShapeDtypeStruct((b, s, d), F32),
        jax.ShapeDtypeStruct((b, s, d), F32),
    )
    out_specs = (row(d), per_group(HEAD_DIM), per_group(HEAD_DIM), transposed, per_group(2 * LANES),
                 transposed, per_group(2 * LANES), per_group(LANES), row(d), row(d), row(d))
    in_specs = [row(d), _const_spec((1, d)), row(LANES), row(LANES),
                _const_spec(wq.shape), _const_spec(wkv.shape), _const_spec(wgn.shape),
                _const_spec(wua.shape), _const_spec(wub.shape), _const_spec(wga.shape),
                _const_spec(wgc.shape)]
    return pl.pallas_call(
        _proj_kernel, grid=grid, in_specs=in_specs, out_specs=out_specs, out_shape=out_shape,
        compiler_params=pltpu.CompilerParams(dimension_semantics=("parallel", "parallel"),
                                             vmem_limit_bytes=VMEM_LIMIT),
        name="proj",
    )(x, norm_g, cosf, sinf, wq, wkv, wgn, wua, wub, wga, wgc)


def _compress_kernel(c_ref, pos_ref, w1_ref, w2_ref, o_ref, *, transpose_out):
    half = CMP_STRIDE * HEAD_DIM
    c = c_ref[0, 0].astype(F32)
    top = (c + pos_ref[:, :half]).astype(BF16)
    bot = (c + pos_ref[:, half:]).astype(BF16)
    a = jnp.dot(top, w1_ref[:half, :], preferred_element_type=F32)
    bm = jnp.dot(bot, w1_ref[half:, :], preferred_element_type=F32)
    n = c.shape[0]
    hid = a + pltpu.roll(bm, n - 1, 0)
    hid = hid * _sigmoid(hid)
    out = jnp.dot(hid.astype(BF16), w2_ref[...], preferred_element_type=F32)
    if transpose_out:
        o_ref[0, 0] = out.T.astype(BF16)
    else:
        lane = lax.broadcasted_iota(jnp.int32, (n, HEAD_DIM), 1)
        ones_col = jnp.where(lane == 0, 1.0, 0.0)
        o_ref[0, 0] = jnp.concatenate([out, ones_col], axis=1).astype(BF16)


def _compress(chunks, pos_flat, w1, w2, transpose_out):
    b, g, n, width = chunks.shape
    if transpose_out:
        out_shape = jax.ShapeDtypeStruct((b, g, HEAD_DIM, n), BF16)
        out_spec = pl.BlockSpec((1, 1, HEAD_DIM, n), lambda bi, gi: (bi, gi, 0, 0))
    else:
        out_shape = jax.ShapeDtypeStruct((b, g, n, LANES), BF16)
        out_spec = pl.BlockSpec((1, 1, n, LANES), lambda bi, gi: (bi, gi, 0, 0))
    return pl.pallas_call(
        functools.partial(_compress_kernel, transpose_out=transpose_out),
        grid=(b, g),
        in_specs=[pl.BlockSpec((1, 1, n, width), lambda bi, gi: (bi, gi, 0, 0)),
                  _const_spec(pos_flat.shape), _const_spec(w1.shape), _const_spec(w2.shape)],
        out_specs=out_spec, out_shape=out_shape,
        compiler_params=pltpu.CompilerParams(dimension_semantics=("parallel", "parallel"),
                                             vmem_limit_bytes=VMEM_LIMIT),
        name="compress_k" if transpose_out else "compress_v",
    )(chunks, pos_flat, w1, w2)


def _nsa_front(t0_cur, t0_nxt, qc_ref, qn_ref, kct_ref, vc_ref, kst_ref, vs_ref, kwt_ref, vw_ref, gates_ref,
               cmask_ref, band_ref, tri_ref, qi_ref, qin_ref, qs_ref, m_ref, acc_ref, part_ref, gsel_ref,
               ocmp_ref, nsel_ref, ecmp_ref, dinv_ref, ewin_ref, first_scores):
    tiles = range(QT_PER_STEP)
    t0 = [pl.multiple_of(t0_cur + h * TQ, TQ) for h in tiles]
    t0n = [pl.multiple_of(t0_nxt + h * TQ, TQ) for h in tiles]
    own = [slice(h * QROWS, (h + 1) * QROWS) for h in tiles]
    head_rows = lambda h, r: slice(h * QROWS + r * TQ, h * QROWS + (r + 1) * TQ)

    o_cmp = [ocmp_ref[own[h], :] for h in tiles]
    not_sel_cur = [nsel_ref[h * TQ:(h + 1) * TQ, :] for h in tiles]

    eye = jnp.where(lax.broadcasted_iota(jnp.int32, (TQ, LANES), 0)
                    == lax.broadcasted_iota(jnp.int32, (TQ, LANES), 1), 1.0, 0.0).astype(BF16)
    for h in tiles:
        qb = qc_ref[0, h * TQ:(h + 1) * TQ, :]
        qbn = qn_ref[0, h * TQ:(h + 1) * TQ, :]
        for r in range(GROUP):
            cols = slice(r * HEAD_DIM, (r + 1) * HEAD_DIM)
            qi_ref[head_rows(h, r), 0:LANES] = eye
            qi_ref[head_rows(h, r), LANES:LANES + HEAD_DIM] = qb[:, cols]
            qin_ref[head_rows(h, r), 0:LANES] = eye
            qin_ref[head_rows(h, r), LANES:LANES + HEAD_DIM] = qbn[:, cols]
            qs_ref[head_rows(h, r), 0:LANES] = not_sel_cur[h]
            qs_ref[head_rows(h, r), LANES:LANES + HEAD_DIM] = qb[:, cols]
    qi = [qi_ref[own[h], :] for h in tiles]
    qin = [qin_ref[own[h], :] for h in tiles]

    w0 = [pl.multiple_of(jnp.maximum(t - WINDOW, 0), LANES) for t in t0]
    shift = [pl.multiple_of(jnp.maximum(WINDOW - t, 0), LANES) for t in t0]
    s_cmp = [jnp.dot(qin[h], jnp.concatenate([cmask_ref[h], kct_ref[0, 0]], axis=0),
                     preferred_element_type=F32) for h in tiles]
    s_dia = [jnp.dot(qi[h], jnp.concatenate([tri_ref[...], kst_ref[0, :, pl.ds(t0[h], TQ)]], axis=0),
                     preferred_element_type=F32) for h in tiles]
    s_win = [jnp.dot(qi[h], jnp.concatenate([band_ref[:, pl.ds(shift[h], WIN_KEYS)],
                                             kwt_ref[0, :, pl.ds(w0[h], WIN_KEYS)]], axis=0),
                     preferred_element_type=F32) for h in tiles]
    first_scores()

    imp_t = []
    for h in tiles:
        for c in range(QROWS // SOFTMAX_ROWS):
            rows = slice(c * SOFTMAX_ROWS, (c + 1) * SOFTMAX_ROWS)
            dst = slice(h * QROWS + rows.start, h * QROWS + rows.stop)
            sc = s_cmp[h][rows, :]
            m = jnp.max(sc, axis=1, keepdims=True)
            e = jnp.exp2(sc - m)
            d = jnp.sum(e, axis=1, keepdims=True)
            dinv = jnp.where(m > -0.5 * MASK_BIG, 1.0 / d, 0.0)
            ecmp_ref[dst, :] = e.astype(BF16)
            dinv_ref[dst, :] = jnp.broadcast_to(dinv, (SOFTMAX_ROWS, LANES))
    for h in tiles:
        both = jnp.dot(ecmp_ref[own[h], :], vc_ref[0, 0], preferred_element_type=F32)
        dinv = dinv_ref[own[h], :]
        ocmp_ref[own[h], :] = both[:, 0:LANES] * dinv
        u = both[:, LANES:2 * LANES] * dinv
        imp = u[0:TQ] + u[TQ:2 * TQ] + u[2 * TQ:3 * TQ] + u[3 * TQ:4 * TQ]
        imp_t.append(imp.T)

    j_i = lax.broadcasted_iota(jnp.int32, (LANES, TQ), 0)
    j_f = j_i.astype(F32)
    tok = lax.broadcasted_iota(jnp.int32, (LANES, TQ), 1)
    work = []
    for h in tiles:
        qblk = (t0n[h] + tok) // SEL_BLOCK
        work.append(jnp.where(j_i >= 1, jnp.where(j_i <= qblk - 2, imp_t[h], -1.0), -1.0))

    def selection_round():
        for h in tiles:
            mx = jnp.max(work[h], axis=0, keepdims=True)
            idx = jnp.min(jnp.where(work[h] == mx, j_f, 1e9), axis=0, keepdims=True)
            pick = j_f == jnp.where(mx >= 0.0, idx, -1.0)
            work[h] = jnp.where(pick, -1.0, work[h])

    def window_chunk(h, c):
        rows = slice(c * SOFTMAX_ROWS, (c + 1) * SOFTMAX_ROWS)
        dst = slice(h * QROWS + rows.start, h * QROWS + rows.stop)
        sw = s_win[h][rows, :]
        ewin_ref[dst, :] = jnp.exp2(sw - jnp.max(sw, axis=1, keepdims=True)).astype(BF16)

    def diagonal(h):
        m0 = jnp.max(s_dia[h], axis=1, keepdims=True)
        pd = jnp.exp2(s_dia[h] - m0)
        acc_ref[own[h], :] = jnp.dot(pd.astype(BF16), vs_ref[0, 0, pl.ds(t0[h], TQ), :],
                                     preferred_element_type=F32)
        m_ref[own[h], :] = jnp.broadcast_to(m0, (QROWS, LANES))

    o_win = []

    def window_pv(h):
        o = jnp.dot(ewin_ref[own[h], :], vw_ref[0, 0, pl.ds(w0[h], WIN_KEYS), :], preferred_element_type=F32)
        o_win.append(o[:, 0:LANES] * (1.0 / o[:, LANES:2 * LANES]))

    fill = []
    for h in tiles:
        fill += [functools.partial(window_chunk, h, c) for c in range(QROWS // SOFTMAX_ROWS)]
        fill.append(functools.partial(window_pv, h))
    fill += [functools.partial(diagonal, h) for h in tiles]
    rounds = SEL_TOPK - N_FORCED
    per_round = -(-len(fill) // rounds)
    for i in range(rounds):
        selection_round()
        for piece in fill[i * per_round:(i + 1) * per_round]:
            piece()
    for h in tiles:
        qblk = (t0n[h] + tok) // SEL_BLOCK
        picked = jnp.where(j_i >= 1, jnp.where(j_i <= qblk - 2, jnp.where(work[h] < 0.0, 1.0, 0.0), 0.0), 0.0)
        taken = jnp.where(j_i == 0, 1.0, jnp.where(j_i == qblk - 1, 1.0, picked))
        not_sel = jnp.where(j_i * SEL_BLOCK < t0n[h], jnp.where(taken > 0.0, 0.0, MASK_BIG), MASK_BIG)
        nsel_ref[h * TQ:(h + 1) * TQ, :] = not_sel.T.astype(BF16)

    for h in tiles:
        gt = gates_ref[0, 0, h * TQ:(h + 1) * TQ, :]
        for r in range(GROUP):
            rows = slice(r * TQ, (r + 1) * TQ)
            part_ref[head_rows(h, r), :] = (gt[:, r:r + 1] * o_cmp[h][rows, :]
                                            + gt[:, 2 * GROUP + r:2 * GROUP + r + 1] * o_win[h][rows, :])
            gsel_ref[head_rows(h, r), :] = jnp.broadcast_to(gt[:, GROUP + r:GROUP + r + 1], (TQ, LANES))


def _nsa_kernel(qc_ref, qn_ref, kct_ref, vc_ref, kst_ref, vs_ref, kwt_ref, vw_ref, gates_ref, nege_ref,
                cmask_ref, band_ref, tri_ref, o_ref, qi_ref, qin_ref, qs_ref, m_ref, acc_ref, part_ref, gsel_ref,
                ocmp_ref, nsel_ref, ecmp_ref, dinv_ref, ewin_ref,
                s0_ref, s1_ref, p0_ref, p1_ref, alpha0_ref, alpha1_ref):
    k = pl.program_id(2)
    n_steps = pl.num_programs(2) - 1

    @pl.when(k == 0)
    def _():
        ocmp_ref[...] = jnp.zeros(ocmp_ref.shape, F32)
        nsel_ref[...] = jnp.zeros(nsel_ref.shape, BF16)

    step_t0 = pl.multiple_of(jnp.maximum(k - 1, 0) * STEP_TOKENS, STEP_TOKENS)
    next_t0 = pl.multiple_of(jnp.minimum(k, n_steps - 1) * STEP_TOKENS, STEP_TOKENS)
    n_tiles = (step_t0 + (QT_PER_STEP - 1) * TQ + KT - 1) // KT
    last_tile = kst_ref.shape[-1] // KT - 1

    def produce(s_ref, c):
        k0 = pl.multiple_of(jnp.minimum(c, last_tile) * KT, KT)
        kaug = jnp.concatenate([nege_ref[:, pl.ds(k0, KT)], kst_ref[0, :, pl.ds(k0, KT)]], axis=0)
        s_ref[...] = jnp.dot(qs_ref[...], kaug, preferred_element_type=F32)

    _nsa_front(step_t0, next_t0, qc_ref, qn_ref, kct_ref, vc_ref, kst_ref, vs_ref, kwt_ref, vw_ref, gates_ref,
               cmask_ref, band_ref, tri_ref, qi_ref, qin_ref, qs_ref, m_ref, acc_ref, part_ref, gsel_ref,
               ocmp_ref, nsel_ref, ecmp_ref, dinv_ref, ewin_ref, first_scores=lambda: produce(s0_ref, 0))

    def consume(buf, c, backwards):
        s_ref, p_ref, alpha_ref = buf
        k0 = pl.multiple_of(c * KT, KT)
        chunks = range(STEP_ROWS // SOFTMAX_ROWS)
        for chunk in (reversed(chunks) if backwards else chunks):
            rows = slice(chunk * SOFTMAX_ROWS, (chunk + 1) * SOFTMAX_ROWS)
            sc = s_ref[rows, :]
            m_old = m_ref[rows, :]
            m_new = jnp.maximum(m_old, jnp.max(sc, axis=1, keepdims=True))
            p_ref[rows, :] = jnp.exp2(sc - jnp.concatenate([m_new] * (KT // LANES), axis=1)).astype(BF16)
            alpha_ref[rows, :] = jnp.exp2(m_old - m_new)
            m_ref[rows, :] = m_new
        alpha = jnp.concatenate([alpha_ref[...]] * 2, axis=1)
        acc_ref[...] = (alpha * acc_ref[...]
                        + jnp.dot(p_ref[...], vs_ref[0, 0, pl.ds(k0, KT), :], preferred_element_type=F32))

    buf0, buf1 = (s0_ref, p0_ref, alpha0_ref), (s1_ref, p1_ref, alpha1_ref)
    def tile_pair(pair, carry):
        c = 2 * pair
        produce(s1_ref, c + 1)
        consume(buf0, c, backwards=False)
        produce(s0_ref, c + 2)
        consume(buf1, c + 1, backwards=True)
        return carry

    lax.fori_loop(0, n_tiles // 2, tile_pair, 0)

    @pl.when(n_tiles % 2 == 1)
    def _():
        consume(buf0, n_tiles - 1, backwards=False)

    o_all = part_ref[...] + gsel_ref[...] * acc_ref[:, 0:LANES] * (1.0 / acc_ref[:, LANES:2 * LANES])
    for h in range(QT_PER_STEP):
        for r in range(GROUP):
            src = slice(h * QROWS + r * TQ, h * QROWS + (r + 1) * TQ)
            o_ref[0, h * TQ:(h + 1) * TQ, r * HEAD_DIM:(r + 1) * HEAD_DIM] = o_all[src, :HEAD_DIM]


def _nsa(q, kct, vc, kst, vs, kwt, vw, gates, nege, cmask, band, tri):
    b, s, d = q.shape
    n_cmp = kct.shape[-1]
    n_steps = s // STEP_TOKENS
    grid = (b, N_KV, n_steps + 1)
    cur = lambda k: jnp.maximum(k - 1, 0)
    nxt = lambda k: jnp.minimum(k, n_steps - 1)
    per_bg = lambda shape: pl.BlockSpec((1, 1) + shape, lambda bi, gi, k: (bi, gi, 0, 0))
    kt_spec = pl.BlockSpec((1, HEAD_DIM, s), lambda bi, gi, k: (bi, gi, 0))
    whole = lambda a: pl.BlockSpec(a.shape, lambda bi, gi, k: (0,) * a.ndim)
    in_specs = [
        pl.BlockSpec((1, STEP_TOKENS, GROUP * HEAD_DIM), lambda bi, gi, k: (bi, cur(k), gi)),
        pl.BlockSpec((1, STEP_TOKENS, GROUP * HEAD_DIM), lambda bi, gi, k: (bi, nxt(k), gi)),
        per_bg((HEAD_DIM, n_cmp)), per_bg((n_cmp, 2 * LANES)),
        kt_spec, per_bg((s, 2 * LANES)), kt_spec, per_bg((s, 2 * LANES)),
        pl.BlockSpec((1, 1, STEP_TOKENS, LANES), lambda bi, gi, k: (bi, gi, cur(k), 0)),
        whole(nege),
        pl.BlockSpec((QT_PER_STEP, TQ, n_cmp), lambda bi, gi, k: (nxt(k), 0, 0)),
        whole(band), whole(tri),
    ]
    return pl.pallas_call(
        _nsa_kernel, grid=grid, in_specs=in_specs,
        out_specs=pl.BlockSpec((1, STEP_TOKENS, GROUP * HEAD_DIM), lambda bi, gi, k: (bi, cur(k), gi)),
        out_shape=jax.ShapeDtypeStruct((b, s, d), F32),
        scratch_shapes=[pltpu.VMEM((STEP_ROWS, LANES + HEAD_DIM), BF16),
                        pltpu.VMEM((STEP_ROWS, LANES + HEAD_DIM), BF16),
                        pltpu.VMEM((STEP_ROWS, LANES + HEAD_DIM), BF16),
                        pltpu.VMEM((STEP_ROWS, LANES), F32),
                        pltpu.VMEM((STEP_ROWS, 2 * LANES), F32),
                        pltpu.VMEM((STEP_ROWS, LANES), F32),
                        pltpu.VMEM((STEP_ROWS, LANES), F32),
                        pltpu.VMEM((STEP_ROWS, LANES), F32),
                        pltpu.VMEM((STEP_TOKENS, LANES), BF16),
                        pltpu.VMEM((STEP_ROWS, n_cmp), BF16),
                        pltpu.VMEM((STEP_ROWS, LANES), F32),
                        pltpu.VMEM((STEP_ROWS, WIN_KEYS), BF16),
                        pltpu.VMEM((STEP_ROWS, KT), F32),
                        pltpu.VMEM((STEP_ROWS, KT), F32),
                        pltpu.VMEM((STEP_ROWS, KT), BF16),
                        pltpu.VMEM((STEP_ROWS, KT), BF16),
                        pltpu.VMEM((STEP_ROWS, LANES), F32),
                        pltpu.VMEM((STEP_ROWS, LANES), F32)],
        compiler_params=pltpu.CompilerParams(dimension_semantics=("parallel", "parallel", "arbitrary"),
                                             vmem_limit_bytes=VMEM_LIMIT),
        name="nsa",
    )(q, q, kct, vc, kst, vs, kwt, vw, gates, nege, cmask, band, tri)


def _mix_kernel(x_ref, z_ref, zprev_ref, attn_ref, ga_ref, gc_ref, cw_ref, cb_ref, lg_ref, lb_ref,
                wco_ref, bco_ref, wout_ref, o_ref, zbuf_ref, zshift_ref):
    tm = z_ref.shape[1]
    halo = zprev_ref[0]
    zbuf_ref[0:CONV_HALO, :] = jnp.where(pl.program_id(1) > 0, halo, jnp.zeros_like(halo))
    zbuf_ref[CONV_HALO:CONV_HALO + tm, :] = z_ref[0]
    first = CONV_HALO - (CONV_WIDTH - 1)
    zc = cb_ref[...]
    for phase in range(SUBLANES):
        taps = [k for k in range(CONV_WIDTH) if (first + k) % SUBLANES == phase]
        span = max(first + k - phase for k in taps) + tm
        zshift_ref[phase, 0:span, :] = zbuf_ref[phase:phase + span, :]
        for k in taps:
            off = first + k - phase
            zc = zc + cw_ref[k:k + 1, :] * zshift_ref[phase, off:off + tm, :]
    mu = jnp.mean(zc, axis=-1, keepdims=True)
    dev = zc - mu
    var = jnp.mean(dev * dev, axis=-1, keepdims=True)
    zn = dev * lax.rsqrt(var + EPS) * lg_ref[...] + lb_ref[...]
    act = (zn * _sigmoid(zn)).astype(BF16)
    conv = jnp.dot(act, wco_ref[...], preferred_element_type=F32) + bco_ref[...]
    merged = (ga_ref[0] * attn_ref[0] + gc_ref[0] * conv).astype(BF16)
    o_ref[0] = x_ref[0] + jnp.dot(merged, wout_ref[...], preferred_element_type=F32)


def _mix(x, z, attn, ga, gc, conv_w, conv_b, ln_g, ln_b, w_co, b_co, w_out, tm):
    b, s, d = x.shape
    row = pl.BlockSpec((1, tm, d), lambda bi, i: (bi, i, 0))
    ratio = tm // CONV_HALO
    prev = pl.BlockSpec((1, CONV_HALO, d), lambda bi, i: (bi, jnp.maximum(i * ratio - 1, 0), 0))
    in_specs = [row, row, prev, row, row, row, _const_spec(conv_w.shape), _const_spec((1, d)),
                _const_spec((1, d)), _const_spec((1, d)), _const_spec(w_co.shape), _const_spec((1, d)),
                _const_spec(w_out.shape)]
    return pl.pallas_call(
        _mix_kernel, grid=(b, s // tm), in_specs=in_specs, out_specs=row,
        out_shape=jax.ShapeDtypeStruct((b, s, d), F32),
        scratch_shapes=[pltpu.VMEM((CONV_HALO + tm, d), F32),
                        pltpu.VMEM((SUBLANES, CONV_HALO + tm, d), F32)],
        compiler_params=pltpu.CompilerParams(dimension_semantics=("parallel", "parallel"),
                                             vmem_limit_bytes=VMEM_LIMIT),
        name="mix",
    )(x, z, z, attn, ga, gc, conv_w, conv_b, ln_g, ln_b, w_co, b_co, w_out)


def _ffn_kernel(x_ref, g_ref, wg_ref, wu_ref, wd_ref, gf_ref, o_ref, *, chunk):
    x = x_ref[...]
    h = (x * lax.rsqrt(jnp.mean(x * x, axis=-1, keepdims=True) + EPS) * g_ref[...]).astype(BF16)
    y = x
    for c in range(D_FF // chunk):
        cols = slice(c * chunk, (c + 1) * chunk)
        gate = jnp.dot(h, wg_ref[:, cols], preferred_element_type=F32)
        up = jnp.dot(h, wu_ref[:, cols], preferred_element_type=F32)
        act = (gate * _sigmoid(gate) * up).astype(BF16)
        y = y + jnp.dot(act, wd_ref[cols, :], preferred_element_type=F32)
    o_ref[...] = y * lax.rsqrt(jnp.mean(y * y, axis=-1, keepdims=True) + EPS) * gf_ref[...]


def _ffn(x, norm_g, w_gate, w_up, w_down, final_g, tm, chunk):
    t, d = x.shape
    row = pl.BlockSpec((tm, d), lambda i: (i, 0))
    in_specs = [row, _const_spec((1, d)), _const_spec(w_gate.shape), _const_spec(w_up.shape),
                _const_spec(w_down.shape), _const_spec((1, d))]
    return pl.pallas_call(
        functools.partial(_ffn_kernel, chunk=chunk), grid=(t // tm,), in_specs=in_specs, out_specs=row,
        out_shape=jax.ShapeDtypeStruct((t, d), F32),
        compiler_params=pltpu.CompilerParams(dimension_semantics=("parallel",),
                                             vmem_limit_bytes=VMEM_LIMIT),
        name="ffn",
    )(x, norm_g, w_gate, w_up, w_down, final_g)


def _rope_tables(positions):
    half = ROPE_DIM // 2
    inv = ROPE_THETA ** (-jnp.arange(0, ROPE_DIM, 2, dtype=F32) / ROPE_DIM)
    ang = positions.astype(F32)[..., None] * inv
    cos, sin = jnp.cos(ang), jnp.sin(ang)
    rest = positions.shape + (HEAD_DIM - 2 * half,)
    cos_h = jnp.concatenate([cos, cos, jnp.ones(rest, F32)], axis=-1)
    sin_h = jnp.concatenate([-sin, sin, jnp.zeros(rest, F32)], axis=-1)
    reps = LANES // HEAD_DIM
    return jnp.tile(cos_h, (1, 1, reps)), jnp.tile(sin_h, (1, 1, reps))


def _mask_tables(s):
    n_cmp = s // CMP_STRIDE
    ks = jnp.arange(n_cmp)[:, None] * CMP_STRIDE
    bs = jnp.arange(LANES)[None, :] * SEL_BLOCK
    overlap = jnp.clip(jnp.minimum(ks + CMP_BLOCK, bs + SEL_BLOCK) - jnp.maximum(ks, bs), 0)
    overlap = (jnp.where(bs < s, overlap, 0).astype(F32) / CMP_STRIDE).astype(BF16)
    masked = lambda visible: jnp.where(visible, 0.0, -MASK_BIG).astype(BF16)
    neg_e = -(jnp.arange(s)[None, :] // SEL_BLOCK == jnp.arange(LANES)[:, None]).astype(BF16)
    tok = jnp.arange(s).reshape(s // TQ, TQ, 1)
    cmask = masked(jnp.arange(n_cmp)[None, None, :] * CMP_STRIDE + (CMP_BLOCK - 1) <= tok)
    tau = jnp.arange(TQ)[:, None]
    col = jnp.arange(WIN_KEYS + WINDOW)[None, :]
    band = masked((col > tau) & (col <= tau + WINDOW))
    tri = masked(jnp.arange(TQ)[None, :] <= tau)
    return overlap, neg_e, cmask, band, tri


def kernel(x, positions, norm_mix_g, w_in, cmp_k_pos, cmp_k_w1, cmp_k_w2, cmp_v_pos, cmp_v_w1, cmp_v_w2,
           conv_w, conv_b, conv_norm_g, conv_norm_b, w_conv_out, b_conv_out, w_out,
           norm_ffn_g, w_ffn_gate, w_ffn_up, w_ffn_down, norm_final_g):
    b, s, d = x.shape
    assert d == D_MODEL and s % KT == 0 and s // SEL_BLOCK <= LANES and s >= WIN_KEYS
    row_vec = lambda v: v.reshape(1, -1).astype(F32)

    o_kv = N_HEADS * HEAD_DIM
    o_gn = o_kv + 6 * KV_COLS
    o_ua = o_gn + 3 * N_HEADS
    wq = w_in[:, :o_kv].astype(BF16)
    wkv = w_in[:, o_kv:o_gn].astype(BF16)
    wgn = w_in[:, o_gn:o_ua].reshape(d, 3, N_KV, GROUP).transpose(0, 2, 1, 3).reshape(d, N_KV, 3 * GROUP)
    wgn = jnp.pad(wgn, ((0, 0), (0, 0), (0, LANES - 3 * GROUP))).reshape(d, N_KV * LANES).astype(BF16)
    wua, wub, wga, wgc = (w_in[:, o_ua + k * d:o_ua + (k + 1) * d].astype(BF16) for k in range(4))

    cosf, sinf = _rope_tables(positions)
    q, kc, vc, kst, vs, kwt, vw, gates, z, ga, gc = _proj(
        x, row_vec(norm_mix_g), cosf, sinf, wq, wkv, wgn, wua, wub, wga, wgc, tm=512)

    n_chunks = s // CMP_STRIDE
    chunk_w = CMP_STRIDE * HEAD_DIM
    kct = _compress(kc.reshape(b, N_KV, n_chunks, chunk_w), cmp_k_pos.reshape(1, -1),
                    cmp_k_w1.astype(BF16), cmp_k_w2.astype(BF16), True)
    vcp = _compress(vc.reshape(b, N_KV, n_chunks, chunk_w), cmp_v_pos.reshape(1, -1),
                    cmp_v_w1.astype(BF16), cmp_v_w2.astype(BF16), False)

    overlap, nege, cmask, band, tri = _mask_tables(s)
    vc_ov = jnp.concatenate([vcp, jnp.broadcast_to(overlap, (b, N_KV) + overlap.shape)], axis=-1)
    attn = _nsa(q, kct, vc_ov, kst, vs, kwt, vw, gates, nege, cmask, band, tri)

    x1 = _mix(x, z, attn, ga, gc, conv_w.reshape(CONV_WIDTH, d), row_vec(conv_b), row_vec(conv_norm_g),
              row_vec(conv_norm_b), w_conv_out.astype(BF16), row_vec(b_conv_out), w_out.astype(BF16), tm=256)

    out = _ffn(x1.reshape(b * s, d), row_vec(norm_ffn_g), w_ffn_gate.astype(BF16), w_ffn_up.astype(BF16),
               w_ffn_down.astype(BF16), row_vec(norm_final_g), tm=512, chunk=D_FF // 2)
    return out.reshape(b, s, d)
```

```python
import functools
import math

import jax
import jax.numpy as jnp
from jax import lax
from jax.experimental import pallas as pl
from jax.experimental.pallas import tpu as pltpu

F32 = jnp.float32
BF16 = jnp.bfloat16

D_MODEL = 1024
N_HEADS = 16
HEAD_DIM = 64
N_KV = 4
GROUP = N_HEADS // N_KV
ROPE_DIM = HEAD_DIM // 4
ROPE_THETA = 500000.0
CMP_BLOCK = 32
CMP_STRIDE = 16
SEL_BLOCK = 64
SEL_TOPK = 16
WINDOW = 512
CONV_WIDTH = 31
D_FF = ((8 * D_MODEL // 3 + 255) // 256) * 256
EPS = 1e-6

LANES = 128
SUBLANES = 8
KV_COLS = N_KV * HEAD_DIM
TQ = LANES
QROWS = GROUP * TQ
QT_PER_STEP = 2
STEP_TOKENS = QT_PER_STEP * TQ
STEP_ROWS = QT_PER_STEP * QROWS
KT = 512
SOFTMAX_ROWS = 64
WIN_KEYS = WINDOW + TQ
N_FORCED = 3
MASK_BIG = 2.0 ** 100
Q_SCALE = HEAD_DIM ** -0.5 * math.log2(math.e)
CONV_HALO = 32
PROJ_ROWS = 512
MIX_ROWS = 512
FFN_ROWS = 512
FFN_CHUNK = D_FF // 2
VMEM_LIMIT = 56 * 1024 * 1024


def _sigmoid(v):
    return 1.0 / (1.0 + jnp.exp(-v))


def _const_spec(shape):
    nd = len(shape)
    return pl.BlockSpec(shape, lambda *_: (0,) * nd, pipeline_mode=pl.Buffered(1))


def _proj_kernel(x_ref, g_ref, cos_ref, sin_ref, wq_ref, wkv_ref, wgn_ref, wua_ref, wub_ref, wga_ref,
                 wgc_ref, q_ref, kc_ref, vc_ref, kst_ref, vs_ref, kwt_ref, vw_ref, gates_ref, z_ref,
                 ga_ref, gc_ref):
    x = x_ref[0]
    h = (x * lax.rsqrt(jnp.mean(x * x, axis=-1, keepdims=True) + EPS) * g_ref[...]).astype(BF16)
    cosf = cos_ref[0]
    sinf = sin_ref[0]

    def rope(y):
        w = y.shape[1]
        lane = lax.broadcasted_iota(jnp.int32, y.shape, 1)
        partner = jnp.where((lane & (HEAD_DIM - 1)) < ROPE_DIM // 2,
                            pltpu.roll(y, w - ROPE_DIM // 2, 1), pltpu.roll(y, ROPE_DIM // 2, 1))
        reps = w // LANES
        return (y * jnp.concatenate([cosf] * reps, axis=1) + partner * jnp.concatenate([sinf] * reps, axis=1))

    def one_head(v_pair, odd):
        lane = lax.broadcasted_iota(jnp.int32, v_pair.shape, 1)
        v = pltpu.roll(v_pair, HEAD_DIM, 1) if odd else v_pair
        return jnp.where(lane < HEAD_DIM, v, 0.0).astype(BF16)

    yq = jnp.dot(h, wq_ref[...], preferred_element_type=F32)
    q_ref[0] = (rope(yq) * Q_SCALE).astype(BF16)

    ykv = jnp.dot(h, wkv_ref[...], preferred_element_type=F32)
    kc = rope(ykv[:, 0 * KV_COLS:1 * KV_COLS])
    vc = ykv[:, 1 * KV_COLS:2 * KV_COLS]
    for g in range(N_KV):
        kc_ref[0, g] = kc[:, g * HEAD_DIM:(g + 1) * HEAD_DIM].astype(BF16)
        vc_ref[0, g] = vc[:, g * HEAD_DIM:(g + 1) * HEAD_DIM].astype(BF16)
    kst_ref[0] = rope(ykv[:, 2 * KV_COLS:3 * KV_COLS]).T.astype(BF16)
    kwt_ref[0] = rope(ykv[:, 4 * KV_COLS:5 * KV_COLS]).T.astype(BF16)
    vs = ykv[:, 3 * KV_COLS:4 * KV_COLS]
    vw = ykv[:, 5 * KV_COLS:6 * KV_COLS]
    for g in range(N_KV):
        pair = slice((g // 2) * LANES, (g // 2 + 1) * LANES)
        ones = jnp.ones((vs.shape[0], LANES), BF16)
        vs_ref[0, g] = jnp.concatenate([one_head(vs[:, pair], g % 2 == 1), ones], axis=1)
        vw_ref[0, g] = jnp.concatenate([one_head(vw[:, pair], g % 2 == 1), ones], axis=1)

    gates = _sigmoid(jnp.dot(h, wgn_ref[...], preferred_element_type=F32))
    for g in range(N_KV):
        gates_ref[0, g] = gates[:, g * LANES:(g + 1) * LANES]

    ua = jnp.dot(h, wua_ref[...], preferred_element_type=F32)
    ub = jnp.dot(h, wub_ref[...], preferred_element_type=F32)
    z_ref[0] = ua * _sigmoid(ub)
    ga_ref[0] = _sigmoid(jnp.dot(h, wga_ref[...], preferred_element_type=F32))
    gc_ref[0] = _sigmoid(jnp.dot(h, wgc_ref[...], preferred_element_type=F32))


def _proj(x, norm_g, cosf, sinf, wq, wkv, wgn, wua, wub, wga, wgc, tm):
    b, s, d = x.shape
    grid = (b, s // tm)
    row = lambda w: pl.BlockSpec((1, tm, w), lambda bi, i: (bi, i, 0))
    per_group = lambda w: pl.BlockSpec((1, N_KV, tm, w), lambda bi, i: (bi, 0, i, 0))
    transposed = pl.BlockSpec((1, KV_COLS, tm), lambda bi, i: (bi, 0, i))
    out_shape = (
        jax.ShapeDtypeStruct((b, s, d), BF16),
        jax.ShapeDtypeStruct((b, N_KV, s, HEAD_DIM), BF16),
        jax.ShapeDtypeStruct((b, N_KV, s, HEAD_DIM), BF16),
        jax.ShapeDtypeStruct((b, KV_COLS, s), BF16),
        jax.ShapeDtypeStruct((b, N_KV, s, 2 * LANES), BF16),
        jax.ShapeDtypeStruct((b, KV_COLS, s), BF16),
        jax.ShapeDtypeStruct((b, N_KV, s, 2 * LANES), BF16),
        jax.ShapeDtypeStruct((b, N_KV, s, LANES), F32),
        jax.ShapeDtypeStruct((b, s, d), F32),
        jax.ShapeDtypeStruct((b, s, d), F32),
        jax.ShapeDtypeStruct((b, s, d), F32),
    )
    out_specs = (row(d), per_group(HEAD_DIM), per_group(HEAD_DIM), transposed, per_group(2 * LANES),
                 transposed, per_group(2 * LANES), per_group(LANES), row(d), row(d), row(d))
    in_specs = [row(d), _const_spec((1, d)), row(LANES), row(LANES),
                _const_spec(wq.shape), _const_spec(wkv.shape), _const_spec(wgn.shape),
                _const_spec(wua.shape), _const_spec(wub.shape), _const_spec(wga.shape),
                _const_spec(wgc.shape)]
    return pl.pallas_call(
        _proj_kernel, grid=grid, in_specs=in_specs, out_specs=out_specs, out_shape=out_shape,
        compiler_params=pltpu.CompilerParams(dimension_semantics=("parallel", "parallel"),
                                             vmem_limit_bytes=VMEM_LIMIT),
        name="proj",
    )(x, norm_g, cosf, sinf, wq, wkv, wgn, wua, wub, wga, wgc)


def _compress_kernel(c_ref, pos_ref, w1_ref, w2_ref, o_ref, *, transpose_out):
    half = CMP_STRIDE * HEAD_DIM
    c = c_ref[0, 0].astype(F32)
    top = (c + pos_ref[:, :half]).astype(BF16)
    bot = (c + pos_ref[:, half:]).astype(BF16)
    a = jnp.dot(top, w1_ref[:half, :], preferred_element_type=F32)
    bm = jnp.dot(bot, w1_ref[half:, :], preferred_element_type=F32)
    n = c.shape[0]
    hid = a + pltpu.roll(bm, n - 1, 0)
    hid = hid * _sigmoid(hid)
    out = jnp.dot(hid.astype(BF16), w2_ref[...], preferred_element_type=F32)
    if transpose_out:
        o_ref[0, 0] = out.T.astype(BF16)
    else:
        lane = lax.broadcasted_iota(jnp.int32, (n, HEAD_DIM), 1)
        ones_col = jnp.where(lane == 0, 1.0, 0.0)
        o_ref[0, 0] = jnp.concatenate([out, ones_col], axis=1).astype(BF16)


def _compress(chunks, pos_flat, w1, w2, transpose_out):
    b, g, n, width = chunks.shape
    if transpose_out:
        out_shape = jax.ShapeDtypeStruct((b, g, HEAD_DIM, n), BF16)
        out_spec = pl.BlockSpec((1, 1, HEAD_DIM, n), lambda bi, gi: (bi, gi, 0, 0))
    else:
        out_shape = jax.ShapeDtypeStruct((b, g, n, LANES), BF16)
        out_spec = pl.BlockSpec((1, 1, n, LANES), lambda bi, gi: (bi, gi, 0, 0))
    return pl.pallas_call(
        functools.partial(_compress_kernel, transpose_out=transpose_out),
        grid=(b, g),
        in_specs=[pl.BlockSpec((1, 1, n, width), lambda bi, gi: (bi, gi, 0, 0)),
                  _const_spec(pos_flat.shape), _const_spec(w1.shape), _const_spec(w2.shape)],
        out_specs=out_spec, out_shape=out_shape,
        compiler_params=pltpu.CompilerParams(dimension_semantics=("parallel", "parallel"),
                                             vmem_limit_bytes=VMEM_LIMIT),
        name="compress_k" if transpose_out else "compress_v",
    )(chunks, pos_flat, w1, w2)


def _nsa_front(t0_cur, t0_nxt, qc_ref, qn_ref, kct_ref, vc_ref, kst_ref, vs_ref, kwt_ref, vw_ref, gates_ref,
               cmask_ref, band_ref, tri_ref, qi_ref, qin_ref, qs_ref, m_ref, acc_ref, part_ref, gsel_ref,
               ocmp_ref, nsel_ref, ecmp_ref, dinv_ref, ewin_ref, first_scores):
    tiles = range(QT_PER_STEP)
    t0 = [pl.multiple_of(t0_cur + h * TQ, TQ) for h in tiles]
    t0n = [pl.multiple_of(t0_nxt + h * TQ, TQ) for h in tiles]
    own = [slice(h * QROWS, (h + 1) * QROWS) for h in tiles]
    head_rows = lambda h, r: slice(h * QROWS + r * TQ, h * QROWS + (r + 1) * TQ)

    o_cmp = [ocmp_ref[own[h], :] for h in tiles]
    not_sel_cur = [nsel_ref[h * TQ:(h + 1) * TQ, :] for h in tiles]

    eye = jnp.where(lax.broadcasted_iota(jnp.int32, (TQ, LANES), 0)
                    == lax.broadcasted_iota(jnp.int32, (TQ, LANES), 1), 1.0, 0.0).astype(BF16)
    for h in tiles:
        qb = qc_ref[0, h * TQ:(h + 1) * TQ, :]
        qbn = qn_ref[0, h * TQ:(h + 1) * TQ, :]
        for r in range(GROUP):
            cols = slice(r * HEAD_DIM, (r + 1) * HEAD_DIM)
            qi_ref[head_rows(h, r), 0:LANES] = eye
            qi_ref[head_rows(h, r), LANES:LANES + HEAD_DIM] = qb[:, cols]
            qin_ref[head_rows(h, r), 0:LANES] = eye
            qin_ref[head_rows(h, r), LANES:LANES + HEAD_DIM] = qbn[:, cols]
            qs_ref[head_rows(h, r), 0:LANES] = not_sel_cur[h]
            qs_ref[head_rows(h, r), LANES:LANES + HEAD_DIM] = qb[:, cols]
    qi = [qi_ref[own[h], :] for h in tiles]
    qin = [qin_ref[own[h], :] for h in tiles]

    w0 = [pl.multiple_of(jnp.maximum(t - WINDOW, 0), LANES) for t in t0]
    shift = [pl.multiple_of(jnp.maximum(WINDOW - t, 0), LANES) for t in t0]
    s_cmp = [jnp.dot(qin[h], jnp.concatenate([cmask_ref[h], kct_ref[0, 0]], axis=0),
                     preferred_element_type=F32) for h in tiles]
    s_dia = [jnp.dot(qi[h], jnp.concatenate([tri_ref[...], kst_ref[0, :, pl.ds(t0[h], TQ)]], axis=0),
                     preferred_element_type=F32) for h in tiles]
    s_win = [jnp.dot(qi[h], jnp.concatenate([band_ref[:, pl.ds(shift[h], WIN_KEYS)],
                                             kwt_ref[0, :, pl.ds(w0[h], WIN_KEYS)]], axis=0),
                     preferred_element_type=F32) for h in tiles]
    first_scores()

    imp_t = []
    for h in tiles:
        for c in range(QROWS // SOFTMAX_ROWS):
            rows = slice(c * SOFTMAX_ROWS, (c + 1) * SOFTMAX_ROWS)
            dst = slice(h * QROWS + rows.start, h * QROWS + rows.stop)
            sc = s_cmp[h][rows, :]
            m = jnp.max(sc, axis=1, keepdims=True)
            e = jnp.exp2(sc - m)
            d = jnp.sum(e, axis=1, keepdims=True)
            dinv = jnp.where(m > -0.5 * MASK_BIG, 1.0 / d, 0.0)
            ecmp_ref[dst, :] = e.astype(BF16)
            dinv_ref[dst, :] = jnp.broadcast_to(dinv, (SOFTMAX_ROWS, LANES))
    for h in tiles:
        both = jnp.dot(ecmp_ref[own[h], :], vc_ref[0, 0], preferred_element_type=F32)
        dinv = dinv_ref[own[h], :]
        ocmp_ref[own[h], :] = both[:, 0:LANES] * dinv
        u = both[:, LANES:2 * LANES] * dinv
        imp = u[0:TQ] + u[TQ:2 * TQ] + u[2 * TQ:3 * TQ] + u[3 * TQ:4 * TQ]
        imp_t.append(imp.T)

    j_i = lax.broadcasted_iota(jnp.int32, (LANES, TQ), 0)
    j_f = j_i.astype(F32)
    tok = lax.broadcasted_iota(jnp.int32, (LANES, TQ), 1)
    work = []
    for h in tiles:
        qblk = (t0n[h] + tok) // SEL_BLOCK
        work.append(jnp.where(j_i >= 1, jnp.where(j_i <= qblk - 2, imp_t[h], -1.0), -1.0))
    for _ in range(SEL_TOPK - N_FORCED):
        for h in tiles:
            mx = jnp.max(work[h], axis=0, keepdims=True)
            idx = jnp.min(jnp.where(work[h] == mx, j_f, 1e9), axis=0, keepdims=True)
            pick = j_f == jnp.where(mx >= 0.0, idx, -1.0)
            work[h] = jnp.where(pick, -1.0, work[h])
    for h in tiles:
        qblk = (t0n[h] + tok) // SEL_BLOCK
        picked = jnp.where(j_i >= 1, jnp.where(j_i <= qblk - 2, jnp.where(work[h] < 0.0, 1.0, 0.0), 0.0), 0.0)
        taken = jnp.where(j_i == 0, 1.0, jnp.where(j_i == qblk - 1, 1.0, picked))
        not_sel = jnp.where(j_i * SEL_BLOCK < t0n[h], jnp.where(taken > 0.0, 0.0, MASK_BIG), MASK_BIG)
        nsel_ref[h * TQ:(h + 1) * TQ, :] = not_sel.T.astype(BF16)

    o_win = []
    for h in tiles:
        for c in range(QROWS // SOFTMAX_ROWS):
            rows = slice(c * SOFTMAX_ROWS, (c + 1) * SOFTMAX_ROWS)
            dst = slice(h * QROWS + rows.start, h * QROWS + rows.stop)
            sw = s_win[h][rows, :]
            ewin_ref[dst, :] = jnp.exp2(sw - jnp.max(sw, axis=1, keepdims=True)).astype(BF16)
    for h in tiles:
        o = jnp.dot(ewin_ref[own[h], :], vw_ref[0, 0, pl.ds(w0[h], WIN_KEYS), :], preferred_element_type=F32)
        o_win.append(o[:, 0:LANES] * (1.0 / o[:, LANES:2 * LANES]))
    for h in tiles:
        m0 = jnp.max(s_dia[h], axis=1, keepdims=True)
        pd = jnp.exp2(s_dia[h] - m0)
        acc_ref[own[h], :] = jnp.dot(pd.astype(BF16), vs_ref[0, 0, pl.ds(t0[h], TQ), :],
                                     preferred_element_type=F32)
        m_ref[own[h], :] = jnp.broadcast_to(m0, (QROWS, LANES))

    for h in tiles:
        gt = gates_ref[0, 0, h * TQ:(h + 1) * TQ, :]
        for r in range(GROUP):
            rows = slice(r * TQ, (r + 1) * TQ)
            part_ref[head_rows(h, r), :] = (gt[:, r:r + 1] * o_cmp[h][rows, :]
                                            + gt[:, 2 * GROUP + r:2 * GROUP + r + 1] * o_win[h][rows, :])
            gsel_ref[head_rows(h, r), :] = jnp.broadcast_to(gt[:, GROUP + r:GROUP + r + 1], (TQ, LANES))


def _nsa_kernel(qc_ref, qn_ref, kct_ref, vc_ref, kst_ref, vs_ref, kwt_ref, vw_ref, gates_ref, nege_ref,
                cmask_ref, band_ref, tri_ref, o_ref, qi_ref, qin_ref, qs_ref, m_ref, acc_ref, part_ref, gsel_ref,
                ocmp_ref, nsel_ref, ecmp_ref, dinv_ref, ewin_ref,
                s0_ref, s1_ref, p0_ref, p1_ref, alpha0_ref, alpha1_ref):
    k = pl.program_id(2)
    n_steps = pl.num_programs(2) - 1

    @pl.when(k == 0)
    def _():
        ocmp_ref[...] = jnp.zeros(ocmp_ref.shape, F32)
        nsel_ref[...] = jnp.zeros(nsel_ref.shape, BF16)

    step_t0 = pl.multiple_of(jnp.maximum(k - 1, 0) * STEP_TOKENS, STEP_TOKENS)
    next_t0 = pl.multiple_of(jnp.minimum(k, n_steps - 1) * STEP_TOKENS, STEP_TOKENS)
    n_tiles = (step_t0 + (QT_PER_STEP - 1) * TQ + KT - 1) // KT
    last_tile = kst_ref.shape[-1] // KT - 1

    def produce(s_ref, c):
        k0 = pl.multiple_of(jnp.minimum(c, last_tile) * KT, KT)
        kaug = jnp.concatenate([nege_ref[:, pl.ds(k0, KT)], kst_ref[0, :, pl.ds(k0, KT)]], axis=0)
        s_ref[...] = jnp.dot(qs_ref[...], kaug, preferred_element_type=F32)

    _nsa_front(step_t0, next_t0, qc_ref, qn_ref, kct_ref, vc_ref, kst_ref, vs_ref, kwt_ref, vw_ref, gates_ref,
               cmask_ref, band_ref, tri_ref, qi_ref, qin_ref, qs_ref, m_ref, acc_ref, part_ref, gsel_ref,
               ocmp_ref, nsel_ref, ecmp_ref, dinv_ref, ewin_ref, first_scores=lambda: produce(s0_ref, 0))

    def consume(buf, c, backwards):
        s_ref, p_ref, alpha_ref = buf
        k0 = pl.multiple_of(c * KT, KT)
        chunks = range(STEP_ROWS // SOFTMAX_ROWS)
        for chunk in (reversed(chunks) if backwards else chunks):
            rows = slice(chunk * SOFTMAX_ROWS, (chunk + 1) * SOFTMAX_ROWS)
            sc = s_ref[rows, :]
            m_old = m_ref[rows, :]
            m_new = jnp.maximum(m_old, jnp.max(sc, axis=1, keepdims=True))
            p_ref[rows, :] = jnp.exp2(sc - jnp.concatenate([m_new] * (KT // LANES), axis=1)).astype(BF16)
            alpha_ref[rows, :] = jnp.exp2(m_old - m_new)
            m_ref[rows, :] = m_new
        alpha = jnp.concatenate([alpha_ref[...]] * 2, axis=1)
        acc_ref[...] = (alpha * acc_ref[...]
                        + jnp.dot(p_ref[...], vs_ref[0, 0, pl.ds(k0, KT), :], preferred_element_type=F32))

    buf0, buf1 = (s0_ref, p0_ref, alpha0_ref), (s1_ref, p1_ref, alpha1_ref)
    def tile_pair(pair, carry):
        c = 2 * pair
        produce(s1_ref, c + 1)
        consume(buf0, c, backwards=False)
        produce(s0_ref, c + 2)
        consume(buf1, c + 1, backwards=True)
        return carry

    lax.fori_loop(0, n_tiles // 2, tile_pair, 0)

    @pl.when(n_tiles % 2 == 1)
    def _():
        consume(buf0, n_tiles - 1, backwards=False)


    o_all = part_ref[...] + gsel_ref[...] * acc_ref[:, 0:LANES] * (1.0 / acc_ref[:, LANES:2 * LANES])
    for h in range(QT_PER_STEP):
        for r in range(GROUP):
            src = slice(h * QROWS + r * TQ, h * QROWS + (r + 1) * TQ)
            o_ref[0, h * TQ:(h + 1) * TQ, r * HEAD_DIM:(r + 1) * HEAD_DIM] = o_all[src, :HEAD_DIM]


def _nsa(q, kct, vc, kst, vs, kwt, vw, gates, nege, cmask, band, tri):
    b, s, d = q.shape
    n_cmp = kct.shape[-1]
    n_steps = s // STEP_TOKENS
    grid = (b, N_KV, n_steps + 1)
    cur = lambda k: jnp.maximum(k - 1, 0)
    nxt = lambda k: jnp.minimum(k, n_steps - 1)
    per_bg = lambda shape: pl.BlockSpec((1, 1) + shape, lambda bi, gi, k: (bi, gi, 0, 0))
    kt_spec = pl.BlockSpec((1, HEAD_DIM, s), lambda bi, gi, k: (bi, gi, 0))
    whole = lambda a: pl.BlockSpec(a.shape, lambda bi, gi, k: (0,) * a.ndim)
    in_specs = [
        pl.BlockSpec((1, STEP_TOKENS, GROUP * HEAD_DIM), lambda bi, gi, k: (bi, cur(k), gi)),
        pl.BlockSpec((1, STEP_TOKENS, GROUP * HEAD_DIM), lambda bi, gi, k: (bi, nxt(k), gi)),
        per_bg((HEAD_DIM, n_cmp)), per_bg((n_cmp, 2 * LANES)),
        kt_spec, per_bg((s, 2 * LANES)), kt_spec, per_bg((s, 2 * LANES)),
        pl.BlockSpec((1, 1, STEP_TOKENS, LANES), lambda bi, gi, k: (bi, gi, cur(k), 0)),
        whole(nege),
        pl.BlockSpec((QT_PER_STEP, TQ, n_cmp), lambda bi, gi, k: (nxt(k), 0, 0)),
        whole(band), whole(tri),
    ]
    return pl.pallas_call(
        _nsa_kernel, grid=grid, in_specs=in_specs,
        out_specs=pl.BlockSpec((1, STEP_TOKENS, GROUP * HEAD_DIM), lambda bi, gi, k: (bi, cur(k), gi)),
        out_shape=jax.ShapeDtypeStruct((b, s, d), F32),
        scratch_shapes=[pltpu.VMEM((STEP_ROWS, LANES + HEAD_DIM), BF16),
                        pltpu.VMEM((STEP_ROWS, LANES + HEAD_DIM), BF16),
                        pltpu.VMEM((STEP_ROWS, LANES + HEAD_DIM), BF16),
                        pltpu.VMEM((STEP_ROWS, LANES), F32),
                        pltpu.VMEM((STEP_ROWS, 2 * LANES), F32),
                        pltpu.VMEM((STEP_ROWS, LANES), F32),
                        pltpu.VMEM((STEP_ROWS, LANES), F32),
                        pltpu.VMEM((STEP_ROWS, LANES), F32),
                        pltpu.VMEM((STEP_TOKENS, LANES), BF16),
                        pltpu.VMEM((STEP_ROWS, n_cmp), BF16),
                        pltpu.VMEM((STEP_ROWS, LANES), F32),
                        pltpu.VMEM((STEP_ROWS, WIN_KEYS), BF16),
                        pltpu.VMEM((STEP_ROWS, KT), F32),
                        pltpu.VMEM((STEP_ROWS, KT), F32),
                        pltpu.VMEM((STEP_ROWS, KT), BF16),
                        pltpu.VMEM((STEP_ROWS, KT), BF16),
                        pltpu.VMEM((STEP_ROWS, LANES), F32),
                        pltpu.VMEM((STEP_ROWS, LANES), F32)],
        compiler_params=pltpu.CompilerParams(dimension_semantics=("parallel", "parallel", "arbitrary"),
                                             vmem_limit_bytes=VMEM_LIMIT),
        name="nsa",
    )(q, q, kct, vc, kst, vs, kwt, vw, gates, nege, cmask, band, tri)


def _mix_kernel(x_ref, z_ref, zprev_ref, attn_ref, ga_ref, gc_ref, cw_ref, cb_ref, lg_ref, lb_ref,
                wco_ref, bco_ref, wout_ref, o_ref, zbuf_ref, zshift_ref):
    tm = z_ref.shape[1]
    halo = zprev_ref[0]
    zbuf_ref[0:CONV_HALO, :] = jnp.where(pl.program_id(1) > 0, halo, jnp.zeros_like(halo))
    zbuf_ref[CONV_HALO:CONV_HALO + tm, :] = z_ref[0]
    first = CONV_HALO - (CONV_WIDTH - 1)
    zc = cb_ref[...]
    for phase in range(SUBLANES):
        taps = [k for k in range(CONV_WIDTH) if (first + k) % SUBLANES == phase]
        span = max(first + k - phase for k in taps) + tm
        zshift_ref[phase, 0:span, :] = zbuf_ref[phase:phase + span, :]
        for k in taps:
            off = first + k - phase
            zc = zc + cw_ref[k:k + 1, :] * zshift_ref[phase, off:off + tm, :]
    mu = jnp.mean(zc, axis=-1, keepdims=True)
    dev = zc - mu
    var = jnp.mean(dev * dev, axis=-1, keepdims=True)
    zn = dev * lax.rsqrt(var + EPS) * lg_ref[...] + lb_ref[...]
    act = (zn * _sigmoid(zn)).astype(BF16)
    conv = jnp.dot(act, wco_ref[...], preferred_element_type=F32) + bco_ref[...]
    merged = (ga_ref[0] * attn_ref[0] + gc_ref[0] * conv).astype(BF16)
    o_ref[0] = x_ref[0] + jnp.dot(merged, wout_ref[...], preferred_element_type=F32)


def _mix(x, z, attn, ga, gc, conv_w, conv_b, ln_g, ln_b, w_co, b_co, w_out, tm):
    b, s, d = x.shape
    row = pl.BlockSpec((1, tm, d), lambda bi, i: (bi, i, 0))
    ratio = tm // CONV_HALO
    prev = pl.BlockSpec((1, CONV_HALO, d), lambda bi, i: (bi, jnp.maximum(i * ratio - 1, 0), 0))
    in_specs = [row, row, prev, row, row, row, _const_spec(conv_w.shape), _const_spec((1, d)),
                _const_spec((1, d)), _const_spec((1, d)), _const_spec(w_co.shape), _const_spec((1, d)),
                _const_spec(w_out.shape)]
    return pl.pallas_call(
        _mix_kernel, grid=(b, s // tm), in_specs=in_specs, out_specs=row,
        out_shape=jax.ShapeDtypeStruct((b, s, d), F32),
        scratch_shapes=[pltpu.VMEM((CONV_HALO + tm, d), F32),
                        pltpu.VMEM((SUBLANES, CONV_HALO + tm, d), F32)],
        compiler_params=pltpu.CompilerParams(dimension_semantics=("parallel", "parallel"),
                                             vmem_limit_bytes=VMEM_LIMIT),
        name="mix",
    )(x, z, z, attn, ga, gc, conv_w, conv_b, ln_g, ln_b, w_co, b_co, w_out)


def _ffn_kernel(x_ref, g_ref, wg_ref, wu_ref, wd_ref, gf_ref, o_ref, *, chunk):
    x = x_ref[...]
    h = (x * lax.rsqrt(jnp.mean(x * x, axis=-1, keepdims=True) + EPS) * g_ref[...]).astype(BF16)
    y = x
    for c in range(D_FF // chunk):
        cols = slice(c * chunk, (c + 1) * chunk)
        gate = jnp.dot(h, wg_ref[:, cols], preferred_element_type=F32)
        up = jnp.dot(h, wu_ref[:, cols], preferred_element_type=F32)
        act = (gate * _sigmoid(gate) * up).astype(BF16)
        y = y + jnp.dot(act, wd_ref[cols, :], preferred_element_type=F32)
    o_ref[...] = y * lax.rsqrt(jnp.mean(y * y, axis=-1, keepdims=True) + EPS) * gf_ref[...]


def _ffn(x, norm_g, w_gate, w_up, w_down, final_g, tm, chunk):
    t, d = x.shape
    row = pl.BlockSpec((tm, d), lambda i: (i, 0))
    in_specs = [row, _const_spec((1, d)), _const_spec(w_gate.shape), _const_spec(w_up.shape),
                _const_spec(w_down.shape), _const_spec((1, d))]
    return pl.pallas_call(
        functools.partial(_ffn_kernel, chunk=chunk), grid=(t // tm,), in_specs=in_specs, out_specs=row,
        out_shape=jax.ShapeDtypeStruct((t, d), F32),
        compiler_params=pltpu.CompilerParams(dimension_semantics=("parallel",),
                                             vmem_limit_bytes=VMEM_LIMIT),
        name="ffn",
    )(x, norm_g, w_gate, w_up, w_down, final_g)


def _rope_tables(positions):
    half = ROPE_DIM // 2
    inv = ROPE_THETA ** (-jnp.arange(0, ROPE_DIM, 2, dtype=F32) / ROPE_DIM)
    ang = positions.astype(F32)[..., None] * inv
    cos, sin = jnp.cos(ang), jnp.sin(ang)
    rest = positions.shape + (HEAD_DIM - 2 * half,)
    cos_h = jnp.concatenate([cos, cos, jnp.ones(rest, F32)], axis=-1)
    sin_h = jnp.concatenate([-sin, sin, jnp.zeros(rest, F32)], axis=-1)
    reps = LANES // HEAD_DIM
    return jnp.tile(cos_h, (1, 1, reps)), jnp.tile(sin_h, (1, 1, reps))


def _mask_tables(s):
    n_cmp = s // CMP_STRIDE
    ks = jnp.arange(n_cmp)[:, None] * CMP_STRIDE
    bs = jnp.arange(LANES)[None, :] * SEL_BLOCK
    overlap = jnp.clip(jnp.minimum(ks + CMP_BLOCK, bs + SEL_BLOCK) - jnp.maximum(ks, bs), 0)
    overlap = (jnp.where(bs < s, overlap, 0).astype(F32) / CMP_STRIDE).astype(BF16)
    masked = lambda visible: jnp.where(visible, 0.0, -MASK_BIG).astype(BF16)
    neg_e = -(jnp.arange(s)[None, :] // SEL_BLOCK == jnp.arange(LANES)[:, None]).astype(BF16)
    tok = jnp.arange(s).reshape(s // TQ, TQ, 1)
    cmask = masked(jnp.arange(n_cmp)[None, None, :] * CMP_STRIDE + (CMP_BLOCK - 1) <= tok)
    tau = jnp.arange(TQ)[:, None]
    col = jnp.arange(WIN_KEYS + WINDOW)[None, :]
    band = masked((col > tau) & (col <= tau + WINDOW))
    tri = masked(jnp.arange(TQ)[None, :] <= tau)
    return overlap, neg_e, cmask, band, tri


def kernel(x, positions, norm_mix_g, w_in, cmp_k_pos, cmp_k_w1, cmp_k_w2, cmp_v_pos, cmp_v_w1, cmp_v_w2,
           conv_w, conv_b, conv_norm_g, conv_norm_b, w_conv_out, b_conv_out, w_out,
           norm_ffn_g, w_ffn_gate, w_ffn_up, w_ffn_down, norm_final_g):
    b, s, d = x.shape
    assert d == D_MODEL and s % KT == 0 and s // SEL_BLOCK <= LANES and s >= WIN_KEYS
    row_vec = lambda v: v.reshape(1, -1).astype(F32)

    o_kv = N_HEADS * HEAD_DIM
    o_gn = o_kv + 6 * KV_COLS
    o_ua = o_gn + 3 * N_HEADS
    wq = w_in[:, :o_kv].astype(BF16)
    wkv = w_in[:, o_kv:o_gn].astype(BF16)
    wgn = w_in[:, o_gn:o_ua].reshape(d, 3, N_KV, GROUP).transpose(0, 2, 1, 3).reshape(d, N_KV, 3 * GROUP)
    wgn = jnp.pad(wgn, ((0, 0), (0, 0), (0, LANES - 3 * GROUP))).reshape(d, N_KV * LANES).astype(BF16)
    wua, wub, wga, wgc = (w_in[:, o_ua + k * d:o_ua + (k + 1) * d].astype(BF16) for k in range(4))

    cosf, sinf = _rope_tables(positions)
    q, kc, vc, kst, vs, kwt, vw, gates, z, ga, gc = _proj(
        x, row_vec(norm_mix_g), cosf, sinf, wq, wkv, wgn, wua, wub, wga, wgc, tm=PROJ_ROWS)

    n_chunks = s // CMP_STRIDE
    chunk_w = CMP_STRIDE * HEAD_DIM
    kct = _compress(kc.reshape(b, N_KV, n_chunks, chunk_w), cmp_k_pos.reshape(1, -1),
                    cmp_k_w1.astype(BF16), cmp_k_w2.astype(BF16), True)
    vcp = _compress(vc.reshape(b, N_KV, n_chunks, chunk_w), cmp_v_pos.reshape(1, -1),
                    cmp_v_w1.astype(BF16), cmp_v_w2.astype(BF16), False)

    overlap, nege, cmask, band, tri = _mask_tables(s)
    vc_ov = jnp.concatenate([vcp, jnp.broadcast_to(overlap, (b, N_KV) + overlap.shape)], axis=-1)
    attn = _nsa(q, kct, vc_ov, kst, vs, kwt, vw, gates, nege, cmask, band, tri)

    x1 = _mix(x, z, attn, ga, gc, conv_w.reshape(CONV_WIDTH, d), row_vec(conv_b), row_vec(conv_norm_g),
              row_vec(conv_norm_b), w_conv_out.astype(BF16), row_vec(b_conv_out), w_out.astype(BF16), tm=MIX_ROWS)

    out = _ffn(x1.reshape(b * s, d), row_vec(norm_ffn_g), w_ffn_gate.astype(BF16), w_ffn_up.astype(BF16),
               w_ffn_down.astype(BF16), row_vec(norm_final_g), tm=FFN_ROWS, chunk=FFN_CHUNK)
    return out.reshape(b, s, d)
```

```python
import functools
import math

import jax
import jax.numpy as jnp
from jax import lax
from jax.experimental import pallas as pl
from jax.experimental.pallas import tpu as pltpu

F32 = jnp.float32
BF16 = jnp.bfloat16

D_MODEL = 1024
N_HEADS = 16
HEAD_DIM = 64
N_KV = 4
GROUP = N_HEADS // N_KV
ROPE_DIM = HEAD_DIM // 4
ROPE_THETA = 500000.0
CMP_BLOCK = 32
CMP_STRIDE = 16
SEL_BLOCK = 64
SEL_TOPK = 16
WINDOW = 512
CONV_WIDTH = 31
D_FF = ((8 * D_MODEL // 3 + 255) // 256) * 256
EPS = 1e-6

LANES = 128
SUBLANES = 8
KV_COLS = N_KV * HEAD_DIM
TQ = LANES
QROWS = GROUP * TQ
QT_PER_STEP = 2
STEP_TOKENS = QT_PER_STEP * TQ
STEP_ROWS = QT_PER_STEP * QROWS
KT = 512
SOFTMAX_ROWS = 64
WIN_KEYS = WINDOW + TQ
N_FORCED = 3
MASK_BIG = 2.0 ** 100
Q_SCALE = HEAD_DIM ** -0.5 * math.log2(math.e)
CONV_HALO = 32
PROJ_ROWS = 512
MIX_ROWS = 512
FFN_ROWS = 512
FFN_CHUNK = D_FF // 2
VMEM_LIMIT = 56 * 1024 * 1024


def _sigmoid(v):
    return 1.0 / (1.0 + jnp.exp(-v))


def _const_spec(shape):
    nd = len(shape)
    return pl.BlockSpec(shape, lambda *_: (0,) * nd, pipeline_mode=pl.Buffered(1))


def _proj_kernel(x_ref, g_ref, cos_ref, sin_ref, wq_ref, wkv_ref, wgn_ref, wua_ref, wub_ref, wga_ref,
                 wgc_ref, q_ref, kc_ref, vc_ref, kst_ref, vs_ref, kwt_ref, vw_ref, gates_ref, z_ref,
                 ga_ref, gc_ref):
    x = x_ref[0]
    h = (x * lax.rsqrt(jnp.mean(x * x, axis=-1, keepdims=True) + EPS) * g_ref[...]).astype(BF16)
    cosf = cos_ref[0]
    sinf = sin_ref[0]

    def rope(y):
        w = y.shape[1]
        lane = lax.broadcasted_iota(jnp.int32, y.shape, 1)
        partner = jnp.where((lane & (HEAD_DIM - 1)) < ROPE_DIM // 2,
                            pltpu.roll(y, w - ROPE_DIM // 2, 1), pltpu.roll(y, ROPE_DIM // 2, 1))
        reps = w // LANES
        return (y * jnp.concatenate([cosf] * reps, axis=1) + partner * jnp.concatenate([sinf] * reps, axis=1))

    def one_head(v_pair, odd):
        lane = lax.broadcasted_iota(jnp.int32, v_pair.shape, 1)
        v = pltpu.roll(v_pair, HEAD_DIM, 1) if odd else v_pair
        return jnp.where(lane < HEAD_DIM, v, 0.0).astype(BF16)

    yq = jnp.dot(h, wq_ref[...], preferred_element_type=F32)
    q_ref[0] = (rope(yq) * Q_SCALE).astype(BF16)

    ykv = jnp.dot(h, wkv_ref[...], preferred_element_type=F32)
    kc = rope(ykv[:, 0 * KV_COLS:1 * KV_COLS])
    vc = ykv[:, 1 * KV_COLS:2 * KV_COLS]
    for g in range(N_KV):
        kc_ref[0, g] = kc[:, g * HEAD_DIM:(g + 1) * HEAD_DIM].astype(BF16)
        vc_ref[0, g] = vc[:, g * HEAD_DIM:(g + 1) * HEAD_DIM].astype(BF16)
    kst_ref[0] = rope(ykv[:, 2 * KV_COLS:3 * KV_COLS]).T.astype(BF16)
    kwt_ref[0] = rope(ykv[:, 4 * KV_COLS:5 * KV_COLS]).T.astype(BF16)
    vs = ykv[:, 3 * KV_COLS:4 * KV_COLS]
    vw = ykv[:, 5 * KV_COLS:6 * KV_COLS]
    for g in range(N_KV):
        pair = slice((g // 2) * LANES, (g // 2 + 1) * LANES)
        ones = jnp.ones((vs.shape[0], LANES), BF16)
        vs_ref[0, g] = jnp.concatenate([one_head(vs[:, pair], g % 2 == 1), ones], axis=1)
        vw_ref[0, g] = jnp.concatenate([one_head(vw[:, pair], g % 2 == 1), ones], axis=1)

    gates = _sigmoid(jnp.dot(h, wgn_ref[...], preferred_element_type=F32))
    for g in range(N_KV):
        gates_ref[0, g] = gates[:, g * LANES:(g + 1) * LANES]

    ua = jnp.dot(h, wua_ref[...], preferred_element_type=F32)
    ub = jnp.dot(h, wub_ref[...], preferred_element_type=F32)
    z_ref[0] = ua * _sigmoid(ub)
    ga_ref[0] = _sigmoid(jnp.dot(h, wga_ref[...], preferred_element_type=F32))
    gc_ref[0] = _sigmoid(jnp.dot(h, wgc_ref[...], preferred_element_type=F32))


def _proj(x, norm_g, cosf, sinf, wq, wkv, wgn, wua, wub, wga, wgc, tm):
    b, s, d = x.shape
    grid = (b, s // tm)
    row = lambda w: pl.BlockSpec((1, tm, w), lambda bi, i: (bi, i, 0))
    per_group = lambda w: pl.BlockSpec((1, N_KV, tm, w), lambda bi, i: (bi, 0, i, 0))
    transposed = pl.BlockSpec((1, KV_COLS, tm), lambda bi, i: (bi, 0, i))
    out_shape = (
        jax.ShapeDtypeStruct((b, s, d), BF16),
        jax.ShapeDtypeStruct((b, N_KV, s, HEAD_DIM), BF16),
        jax.ShapeDtypeStruct((b, N_KV, s, HEAD_DIM), BF16),
        jax.ShapeDtypeStruct((b, KV_COLS, s), BF16),
        jax.ShapeDtypeStruct((b, N_KV, s, 2 * LANES), BF16),
        jax.ShapeDtypeStruct((b, KV_COLS, s), BF16),
        jax.ShapeDtypeStruct((b, N_KV, s, 2 * LANES), BF16),
        jax.ShapeDtypeStruct((b, N_KV, s, LANES), F32),
        jax.ShapeDtypeStruct((b, s, d), F32),
        jax.ShapeDtypeStruct((b, s, d), F32),
        jax.ShapeDtypeStruct((b, s, d), F32),
    )
    out_specs = (row(d), per_group(HEAD_DIM), per_group(HEAD_DIM), transposed, per_group(2 * LANES),
                 transposed, per_group(2 * LANES), per_group(LANES), row(d), row(d), row(d))
    in_specs = [row(d), _const_spec((1, d)), row(LANES), row(LANES),
                _const_spec(wq.shape), _const_spec(wkv.shape), _const_spec(wgn.shape),
                _const_spec(wua.shape), _const_spec(wub.shape), _const_spec(wga.shape),
                _const_spec(wgc.shape)]
    return pl.pallas_call(
        _proj_kernel, grid=grid, in_specs=in_specs, out_specs=out_specs, out_shape=out_shape,
        compiler_params=pltpu.CompilerParams(dimension_semantics=("parallel", "parallel"),
                                             vmem_limit_bytes=VMEM_LIMIT),
        name="proj",
    )(x, norm_g, cosf, sinf, wq, wkv, wgn, wua, wub, wga, wgc)


def _compress_kernel(c_ref, pos_ref, w1_ref, w2_ref, o_ref, *, transpose_out):
    half = CMP_STRIDE * HEAD_DIM
    c = c_ref[0, 0].astype(F32)
    top = (c + pos_ref[:, :half]).astype(BF16)
    bot = (c + pos_ref[:, half:]).astype(BF16)
    a = jnp.dot(top, w1_ref[:half, :], preferred_element_type=F32)
    bm = jnp.dot(bot, w1_ref[half:, :], preferred_element_type=F32)
    n = c.shape[0]
    hid = a + pltpu.roll(bm, n - 1, 0)
    hid = hid * _sigmoid(hid)
    out = jnp.dot(hid.astype(BF16), w2_ref[...], preferred_element_type=F32)
    if transpose_out:
        o_ref[0, 0] = out.T.astype(BF16)
    else:
        lane = lax.broadcasted_iota(jnp.int32, (n, HEAD_DIM), 1)
        ones_col = jnp.where(lane == 0, 1.0, 0.0)
        o_ref[0, 0] = jnp.concatenate([out, ones_col], axis=1).astype(BF16)


def _compress(chunks, pos_flat, w1, w2, transpose_out):
    b, g, n, width = chunks.shape
    if transpose_out:
        out_shape = jax.ShapeDtypeStruct((b, g, HEAD_DIM, n), BF16)
        out_spec = pl.BlockSpec((1, 1, HEAD_DIM, n), lambda bi, gi: (bi, gi, 0, 0))
    else:
        out_shape = jax.ShapeDtypeStruct((b, g, n, LANES), BF16)
        out_spec = pl.BlockSpec((1, 1, n, LANES), lambda bi, gi: (bi, gi, 0, 0))
    return pl.pallas_call(
        functools.partial(_compress_kernel, transpose_out=transpose_out),
        grid=(b, g),
        in_specs=[pl.BlockSpec((1, 1, n, width), lambda bi, gi: (bi, gi, 0, 0)),
                  _const_spec(pos_flat.shape), _const_spec(w1.shape), _const_spec(w2.shape)],
        out_specs=out_spec, out_shape=out_shape,
        compiler_params=pltpu.CompilerParams(dimension_semantics=("parallel", "parallel"),
                                             vmem_limit_bytes=VMEM_LIMIT),
        name="compress_k" if transpose_out else "compress_v",
    )(chunks, pos_flat, w1, w2)


def _nsa_front(t0_cur, t0_nxt, qc_ref, qn_ref, kct_ref, vc_ref, kst_ref, vs_ref, kwt_ref, vw_ref, gates_ref,
               cmask_ref, band_ref, tri_ref, qi_ref, qin_ref, qs_ref, m_ref, acc_ref, part_ref, gsel_ref,
               ocmp_ref, nsel_ref, ecmp_ref, dinv_ref, ewin_ref, first_scores):
    tiles = range(QT_PER_STEP)
    t0 = [pl.multiple_of(t0_cur + h * TQ, TQ) for h in tiles]
    t0n = [pl.multiple_of(t0_nxt + h * TQ, TQ) for h in tiles]
    own = [slice(h * QROWS, (h + 1) * QROWS) for h in tiles]
    head_rows = lambda h, r: slice(h * QROWS + r * TQ, h * QROWS + (r + 1) * TQ)

    o_cmp = [ocmp_ref[own[h], :] for h in tiles]
    not_sel_cur = [nsel_ref[h * TQ:(h + 1) * TQ, :] for h in tiles]

    eye = jnp.where(lax.broadcasted_iota(jnp.int32, (TQ, LANES), 0)
                    == lax.broadcasted_iota(jnp.int32, (TQ, LANES), 1), 1.0, 0.0).astype(BF16)
    for h in tiles:
        qb = qc_ref[0, h * TQ:(h + 1) * TQ, :]
        qbn = qn_ref[0, h * TQ:(h + 1) * TQ, :]
        for r in range(GROUP):
            cols = slice(r * HEAD_DIM, (r + 1) * HEAD_DIM)
            qi_ref[head_rows(h, r), 0:LANES] = eye
            qi_ref[head_rows(h, r), LANES:LANES + HEAD_DIM] = qb[:, cols]
            qin_ref[head_rows(h, r), 0:LANES] = eye
            qin_ref[head_rows(h, r), LANES:LANES + HEAD_DIM] = qbn[:, cols]
            qs_ref[head_rows(h, r), 0:LANES] = not_sel_cur[h]
            qs_ref[head_rows(h, r), LANES:LANES + HEAD_DIM] = qb[:, cols]
    qi = [qi_ref[own[h], :] for h in tiles]
    qin = [qin_ref[own[h], :] for h in tiles]

    w0 = [pl.multiple_of(jnp.maximum(t - WINDOW, 0), LANES) for t in t0]
    shift = [pl.multiple_of(jnp.maximum(WINDOW - t, 0), LANES) for t in t0]
    s_cmp = [jnp.dot(qin[h], jnp.concatenate([cmask_ref[h], kct_ref[0, 0]], axis=0),
                     preferred_element_type=F32) for h in tiles]
    s_win, s_dia = [], []
    for h in tiles:
        masks = jnp.concatenate([band_ref[:, pl.ds(shift[h], WIN_KEYS)], tri_ref[...]], axis=1)
        keys = jnp.concatenate([kwt_ref[0, :, pl.ds(w0[h], WIN_KEYS)], kst_ref[0, :, pl.ds(t0[h], TQ)]], axis=1)
        both = jnp.dot(qi[h], jnp.concatenate([masks, keys], axis=0), preferred_element_type=F32)
        s_win.append(both[:, 0:WIN_KEYS])
        s_dia.append(both[:, WIN_KEYS:WIN_KEYS + TQ])
    first_scores()

    imp_t = []
    for h in tiles:
        for c in range(QROWS // SOFTMAX_ROWS):
            rows = slice(c * SOFTMAX_ROWS, (c + 1) * SOFTMAX_ROWS)
            dst = slice(h * QROWS + rows.start, h * QROWS + rows.stop)
            sc = s_cmp[h][rows, :]
            m = jnp.max(sc, axis=1, keepdims=True)
            e = jnp.exp2(sc - m)
            d = jnp.sum(e, axis=1, keepdims=True)
            dinv = jnp.where(m > -0.5 * MASK_BIG, 1.0 / d, 0.0)
            ecmp_ref[dst, :] = e.astype(BF16)
            dinv_ref[dst, :] = jnp.broadcast_to(dinv, (SOFTMAX_ROWS, LANES))
    for h in tiles:
        both = jnp.dot(ecmp_ref[own[h], :], vc_ref[0, 0], preferred_element_type=F32)
        dinv = dinv_ref[own[h], :]
        ocmp_ref[own[h], :] = both[:, 0:LANES] * dinv
        u = both[:, LANES:2 * LANES] * dinv
        imp = u[0:TQ] + u[TQ:2 * TQ] + u[2 * TQ:3 * TQ] + u[3 * TQ:4 * TQ]
        imp_t.append(imp.T)

    j_i = lax.broadcasted_iota(jnp.int32, (LANES, TQ), 0)
    j_f = j_i.astype(F32)
    tok = lax.broadcasted_iota(jnp.int32, (LANES, TQ), 1)
    work = []
    for h in tiles:
        qblk = (t0n[h] + tok) // SEL_BLOCK
        work.append(jnp.where(j_i >= 1, jnp.where(j_i <= qblk - 2, imp_t[h], -1.0), -1.0))
    for _ in range(SEL_TOPK - N_FORCED):
        for h in tiles:
            mx = jnp.max(work[h], axis=0, keepdims=True)
            idx = jnp.min(jnp.where(work[h] == mx, j_f, 1e9), axis=0, keepdims=True)
            pick = j_f == jnp.where(mx >= 0.0, idx, -1.0)
            work[h] = jnp.where(pick, -1.0, work[h])
    for h in tiles:
        qblk = (t0n[h] + tok) // SEL_BLOCK
        picked = jnp.where(j_i >= 1, jnp.where(j_i <= qblk - 2, jnp.where(work[h] < 0.0, 1.0, 0.0), 0.0), 0.0)
        taken = jnp.where(j_i == 0, 1.0, jnp.where(j_i == qblk - 1, 1.0, picked))
        not_sel = jnp.where(j_i * SEL_BLOCK < t0n[h], jnp.where(taken > 0.0, 0.0, MASK_BIG), MASK_BIG)
        nsel_ref[h * TQ:(h + 1) * TQ, :] = not_sel.T.astype(BF16)

    o_win = []
    for h in tiles:
        for c in range(QROWS // SOFTMAX_ROWS):
            rows = slice(c * SOFTMAX_ROWS, (c + 1) * SOFTMAX_ROWS)
            dst = slice(h * QROWS + rows.start, h * QROWS + rows.stop)
            sw = s_win[h][rows, :]
            ewin_ref[dst, :] = jnp.exp2(sw - jnp.max(sw, axis=1, keepdims=True)).astype(BF16)
    for h in tiles:
        o = jnp.dot(ewin_ref[own[h], :], vw_ref[0, 0, pl.ds(w0[h], WIN_KEYS), :], preferred_element_type=F32)
        o_win.append(o[:, 0:LANES] * (1.0 / o[:, LANES:2 * LANES]))
    for h in tiles:
        m0 = jnp.max(s_dia[h], axis=1, keepdims=True)
        pd = jnp.exp2(s_dia[h] - m0)
        acc_ref[own[h], :] = jnp.dot(pd.astype(BF16), vs_ref[0, 0, pl.ds(t0[h], TQ), :],
                                     preferred_element_type=F32)
        m_ref[own[h], :] = jnp.broadcast_to(m0, (QROWS, LANES))

    for h in tiles:
        gt = gates_ref[0, 0, h * TQ:(h + 1) * TQ, :]
        for r in range(GROUP):
            rows = slice(r * TQ, (r + 1) * TQ)
            part_ref[head_rows(h, r), :] = (gt[:, r:r + 1] * o_cmp[h][rows, :]
                                            + gt[:, 2 * GROUP + r:2 * GROUP + r + 1] * o_win[h][rows, :])
            gsel_ref[head_rows(h, r), :] = jnp.broadcast_to(gt[:, GROUP + r:GROUP + r + 1], (TQ, LANES))


def _nsa_kernel(qc_ref, qn_ref, kct_ref, vc_ref, kst_ref, vs_ref, kwt_ref, vw_ref, gates_ref, nege_ref,
                cmask_ref, band_ref, tri_ref, o_ref, qi_ref, qin_ref, qs_ref, m_ref, acc_ref, part_ref, gsel_ref,
                ocmp_ref, nsel_ref, ecmp_ref, dinv_ref, ewin_ref,
                s0_ref, s1_ref, p0_ref, p1_ref, alpha0_ref, alpha1_ref):
    k = pl.program_id(2)
    n_steps = pl.num_programs(2) - 1

    @pl.when(k == 0)
    def _():
        ocmp_ref[...] = jnp.zeros(ocmp_ref.shape, F32)
        nsel_ref[...] = jnp.zeros(nsel_ref.shape, BF16)

    step_t0 = pl.multiple_of(jnp.maximum(k - 1, 0) * STEP_TOKENS, STEP_TOKENS)
    next_t0 = pl.multiple_of(jnp.minimum(k, n_steps - 1) * STEP_TOKENS, STEP_TOKENS)
    n_tiles = (step_t0 + (QT_PER_STEP - 1) * TQ + KT - 1) // KT
    last_tile = kst_ref.shape[-1] // KT - 1

    def produce(s_ref, c):
        k0 = pl.multiple_of(jnp.minimum(c, last_tile) * KT, KT)
        kaug = jnp.concatenate([nege_ref[:, pl.ds(k0, KT)], kst_ref[0, :, pl.ds(k0, KT)]], axis=0)
        s_ref[...] = jnp.dot(qs_ref[...], kaug, preferred_element_type=F32)

    _nsa_front(step_t0, next_t0, qc_ref, qn_ref, kct_ref, vc_ref, kst_ref, vs_ref, kwt_ref, vw_ref, gates_ref,
               cmask_ref, band_ref, tri_ref, qi_ref, qin_ref, qs_ref, m_ref, acc_ref, part_ref, gsel_ref,
               ocmp_ref, nsel_ref, ecmp_ref, dinv_ref, ewin_ref, first_scores=lambda: produce(s0_ref, 0))

    def consume(buf, c, backwards):
        s_ref, p_ref, alpha_ref = buf
        k0 = pl.multiple_of(c * KT, KT)
        chunks = range(STEP_ROWS // SOFTMAX_ROWS)
        for chunk in (reversed(chunks) if backwards else chunks):
            rows = slice(chunk * SOFTMAX_ROWS, (chunk + 1) * SOFTMAX_ROWS)
            sc = s_ref[rows, :]
            m_old = m_ref[rows, :]
            m_new = jnp.maximum(m_old, jnp.max(sc, axis=1, keepdims=True))
            p_ref[rows, :] = jnp.exp2(sc - jnp.concatenate([m_new] * (KT // LANES), axis=1)).astype(BF16)
            alpha_ref[rows, :] = jnp.exp2(m_old - m_new)
            m_ref[rows, :] = m_new
        alpha = jnp.concatenate([alpha_ref[...]] * 2, axis=1)
        acc_ref[...] = (alpha * acc_ref[...]
                        + jnp.dot(p_ref[...], vs_ref[0, 0, pl.ds(k0, KT), :], preferred_element_type=F32))

    buf0, buf1 = (s0_ref, p0_ref, alpha0_ref), (s1_ref, p1_ref, alpha1_ref)
    def tile_pair(pair, carry):
        c = 2 * pair
        produce(s1_ref, c + 1)
        consume(buf0, c, backwards=False)
        produce(s0_ref, c + 2)
        consume(buf1, c + 1, backwards=True)
        return carry

    lax.fori_loop(0, n_tiles // 2, tile_pair, 0)

    @pl.when(n_tiles % 2 == 1)
    def _():
        consume(buf0, n_tiles - 1, backwards=False)


    o_all = part_ref[...] + gsel_ref[...] * acc_ref[:, 0:LANES] * (1.0 / acc_ref[:, LANES:2 * LANES])
    for h in range(QT_PER_STEP):
        for r in range(GROUP):
            src = slice(h * QROWS + r * TQ, h * QROWS + (r + 1) * TQ)
            o_ref[0, h * TQ:(h + 1) * TQ, r * HEAD_DIM:(r + 1) * HEAD_DIM] = o_all[src, :HEAD_DIM]


def _nsa(q, kct, vc, kst, vs, kwt, vw, gates, nege, cmask, band, tri):
    b, s, d = q.shape
    n_cmp = kct.shape[-1]
    n_steps = s // STEP_TOKENS
    grid = (b, N_KV, n_steps + 1)
    cur = lambda k: jnp.maximum(k - 1, 0)
    nxt = lambda k: jnp.minimum(k, n_steps - 1)
    per_bg = lambda shape: pl.BlockSpec((1, 1) + shape, lambda bi, gi, k: (bi, gi, 0, 0))
    kt_spec = pl.BlockSpec((1, HEAD_DIM, s), lambda bi, gi, k: (bi, gi, 0))
    whole = lambda a: pl.BlockSpec(a.shape, lambda bi, gi, k: (0,) * a.ndim)
    in_specs = [
        pl.BlockSpec((1, STEP_TOKENS, GROUP * HEAD_DIM), lambda bi, gi, k: (bi, cur(k), gi)),
        pl.BlockSpec((1, STEP_TOKENS, GROUP * HEAD_DIM), lambda bi, gi, k: (bi, nxt(k), gi)),
        per_bg((HEAD_DIM, n_cmp)), per_bg((n_cmp, 2 * LANES)),
        kt_spec, per_bg((s, 2 * LANES)), kt_spec, per_bg((s, 2 * LANES)),
        pl.BlockSpec((1, 1, STEP_TOKENS, LANES), lambda bi, gi, k: (bi, gi, cur(k), 0)),
        whole(nege),
        pl.BlockSpec((QT_PER_STEP, TQ, n_cmp), lambda bi, gi, k: (nxt(k), 0, 0)),
        whole(band), whole(tri),
    ]
    return pl.pallas_call(
        _nsa_kernel, grid=grid, in_specs=in_specs,
        out_specs=pl.BlockSpec((1, STEP_TOKENS, GROUP * HEAD_DIM), lambda bi, gi, k: (bi, cur(k), gi)),
        out_shape=jax.ShapeDtypeStruct((b, s, d), F32),
        scratch_shapes=[pltpu.VMEM((STEP_ROWS, LANES + HEAD_DIM), BF16),
                        pltpu.VMEM((STEP_ROWS, LANES + HEAD_DIM), BF16),
                        pltpu.VMEM((STEP_ROWS, LANES + HEAD_DIM), BF16),
                        pltpu.VMEM((STEP_ROWS, LANES), F32),
                        pltpu.VMEM((STEP_ROWS, 2 * LANES), F32),
                        pltpu.VMEM((STEP_ROWS, LANES), F32),
                        pltpu.VMEM((STEP_ROWS, LANES), F32),
                        pltpu.VMEM((STEP_ROWS, LANES), F32),
                        pltpu.VMEM((STEP_TOKENS, LANES), BF16),
                        pltpu.VMEM((STEP_ROWS, n_cmp), BF16),
                        pltpu.VMEM((STEP_ROWS, LANES), F32),
                        pltpu.VMEM((STEP_ROWS, WIN_KEYS), BF16),
                        pltpu.VMEM((STEP_ROWS, KT), F32),
                        pltpu.VMEM((STEP_ROWS, KT), F32),
                        pltpu.VMEM((STEP_ROWS, KT), BF16),
                        pltpu.VMEM((STEP_ROWS, KT), BF16),
                        pltpu.VMEM((STEP_ROWS, LANES), F32),
                        pltpu.VMEM((STEP_ROWS, LANES), F32)],
        compiler_params=pltpu.CompilerParams(dimension_semantics=("parallel", "parallel", "arbitrary"),
                                             vmem_limit_bytes=VMEM_LIMIT),
        name="nsa",
    )(q, q, kct, vc, kst, vs, kwt, vw, gates, nege, cmask, band, tri)


def _mix_kernel(x_ref, z_ref, zprev_ref, attn_ref, ga_ref, gc_ref, cw_ref, cb_ref, lg_ref, lb_ref,
                wco_ref, bco_ref, wout_ref, o_ref, zbuf_ref, zshift_ref):
    tm = z_ref.shape[1]
    halo = zprev_ref[0]
    zbuf_ref[0:CONV_HALO, :] = jnp.where(pl.program_id(1) > 0, halo, jnp.zeros_like(halo))
    zbuf_ref[CONV_HALO:CONV_HALO + tm, :] = z_ref[0]
    first = CONV_HALO - (CONV_WIDTH - 1)
    zc = cb_ref[...]
    for phase in range(SUBLANES):
        taps = [k for k in range(CONV_WIDTH) if (first + k) % SUBLANES == phase]
        span = max(first + k - phase for k in taps) + tm
        zshift_ref[phase, 0:span, :] = zbuf_ref[phase:phase + span, :]
        for k in taps:
            off = first + k - phase
            zc = zc + cw_ref[k:k + 1, :] * zshift_ref[phase, off:off + tm, :]
    mu = jnp.mean(zc, axis=-1, keepdims=True)
    dev = zc - mu
    var = jnp.mean(dev * dev, axis=-1, keepdims=True)
    zn = dev * lax.rsqrt(var + EPS) * lg_ref[...] + lb_ref[...]
    act = (zn * _sigmoid(zn)).astype(BF16)
    conv = jnp.dot(act, wco_ref[...], preferred_element_type=F32) + bco_ref[...]
    merged = (ga_ref[0] * attn_ref[0] + gc_ref[0] * conv).astype(BF16)
    o_ref[0] = x_ref[0] + jnp.dot(merged, wout_ref[...], preferred_element_type=F32)


def _mix(x, z, attn, ga, gc, conv_w, conv_b, ln_g, ln_b, w_co, b_co, w_out, tm):
    b, s, d = x.shape
    row = pl.BlockSpec((1, tm, d), lambda bi, i: (bi, i, 0))
    ratio = tm // CONV_HALO
    prev = pl.BlockSpec((1, CONV_HALO, d), lambda bi, i: (bi, jnp.maximum(i * ratio - 1, 0), 0))
    in_specs = [row, row, prev, row, row, row, _const_spec(conv_w.shape), _const_spec((1, d)),
                _const_spec((1, d)), _const_spec((1, d)), _const_spec(w_co.shape), _const_spec((1, d)),
                _const_spec(w_out.shape)]
    return pl.pallas_call(
        _mix_kernel, grid=(b, s // tm), in_specs=in_specs, out_specs=row,
        out_shape=jax.ShapeDtypeStruct((b, s, d), F32),
        scratch_shapes=[pltpu.VMEM((CONV_HALO + tm, d), F32),
                        pltpu.VMEM((SUBLANES, CONV_HALO + tm, d), F32)],
        compiler_params=pltpu.CompilerParams(dimension_semantics=("parallel", "parallel"),
                                             vmem_limit_bytes=VMEM_LIMIT),
        name="mix",
    )(x, z, z, attn, ga, gc, conv_w, conv_b, ln_g, ln_b, w_co, b_co, w_out)


def _ffn_kernel(x_ref, g_ref, wg_ref, wu_ref, wd_ref, gf_ref, o_ref, *, chunk):
    x = x_ref[...]
    h = (x * lax.rsqrt(jnp.mean(x * x, axis=-1, keepdims=True) + EPS) * g_ref[...]).astype(BF16)
    y = x
    for c in range(D_FF // chunk):
        cols = slice(c * chunk, (c + 1) * chunk)
        gate = jnp.dot(h, wg_ref[:, cols], preferred_element_type=F32)
        up = jnp.dot(h, wu_ref[:, cols], preferred_element_type=F32)
        act = (gate * _sigmoid(gate) * up).astype(BF16)
        y = y + jnp.dot(act, wd_ref[cols, :], preferred_element_type=F32)
    o_ref[...] = y * lax.rsqrt(jnp.mean(y * y, axis=-1, keepdims=True) + EPS) * gf_ref[...]


def _ffn(x, norm_g, w_gate, w_up, w_down, final_g, tm, chunk):
    t, d = x.shape
    row = pl.BlockSpec((tm, d), lambda i: (i, 0))
    in_specs = [row, _const_spec((1, d)), _const_spec(w_gate.shape), _const_spec(w_up.shape),
                _const_spec(w_down.shape), _const_spec((1, d))]
    return pl.pallas_call(
        functools.partial(_ffn_kernel, chunk=chunk), grid=(t // tm,), in_specs=in_specs, out_specs=row,
        out_shape=jax.ShapeDtypeStruct((t, d), F32),
        compiler_params=pltpu.CompilerParams(dimension_semantics=("parallel",),
                                             vmem_limit_bytes=VMEM_LIMIT),
        name="ffn",
    )(x, norm_g, w_gate, w_up, w_down, final_g)


def _rope_tables(positions):
    half = ROPE_DIM // 2
    inv = ROPE_THETA ** (-jnp.arange(0, ROPE_DIM, 2, dtype=F32) / ROPE_DIM)
    ang = positions.astype(F32)[..., None] * inv
    cos, sin = jnp.cos(ang), jnp.sin(ang)
    rest = positions.shape + (HEAD_DIM - 2 * half,)
    cos_h = jnp.concatenate([cos, cos, jnp.ones(rest, F32)], axis=-1)
    sin_h = jnp.concatenate([-sin, sin, jnp.zeros(rest, F32)], axis=-1)
    reps = LANES // HEAD_DIM
    return jnp.tile(cos_h, (1, 1, reps)), jnp.tile(sin_h, (1, 1, reps))


def _mask_tables(s):
    n_cmp = s // CMP_STRIDE
    ks = jnp.arange(n_cmp)[:, None] * CMP_STRIDE
    bs = jnp.arange(LANES)[None, :] * SEL_BLOCK
    overlap = jnp.clip(jnp.minimum(ks + CMP_BLOCK, bs + SEL_BLOCK) - jnp.maximum(ks, bs), 0)
    overlap = (jnp.where(bs < s, overlap, 0).astype(F32) / CMP_STRIDE).astype(BF16)
    masked = lambda visible: jnp.where(visible, 0.0, -MASK_BIG).astype(BF16)
    neg_e = -(jnp.arange(s)[None, :] // SEL_BLOCK == jnp.arange(LANES)[:, None]).astype(BF16)
    tok = jnp.arange(s).reshape(s // TQ, TQ, 1)
    cmask = masked(jnp.arange(n_cmp)[None, None, :] * CMP_STRIDE + (CMP_BLOCK - 1) <= tok)
    tau = jnp.arange(TQ)[:, None]
    col = jnp.arange(WIN_KEYS + WINDOW)[None, :]
    band = masked((col > tau) & (col <= tau + WINDOW))
    tri = masked(jnp.arange(TQ)[None, :] <= tau)
    return overlap, neg_e, cmask, band, tri


def kernel(x, positions, norm_mix_g, w_in, cmp_k_pos, cmp_k_w1, cmp_k_w2, cmp_v_pos, cmp_v_w1, cmp_v_w2,
           conv_w, conv_b, conv_norm_g, conv_norm_b, w_conv_out, b_conv_out, w_out,
           norm_ffn_g, w_ffn_gate, w_ffn_up, w_ffn_down, norm_final_g):
    b, s, d = x.shape
    assert d == D_MODEL and s % KT == 0 and s // SEL_BLOCK <= LANES and s >= WIN_KEYS
    row_vec = lambda v: v.reshape(1, -1).astype(F32)

    o_kv = N_HEADS * HEAD_DIM
    o_gn = o_kv + 6 * KV_COLS
    o_ua = o_gn + 3 * N_HEADS
    wq = w_in[:, :o_kv].astype(BF16)
    wkv = w_in[:, o_kv:o_gn].astype(BF16)
    wgn = w_in[:, o_gn:o_ua].reshape(d, 3, N_KV, GROUP).transpose(0, 2, 1, 3).reshape(d, N_KV, 3 * GROUP)
    wgn = jnp.pad(wgn, ((0, 0), (0, 0), (0, LANES - 3 * GROUP))).reshape(d, N_KV * LANES).astype(BF16)
    wua, wub, wga, wgc = (w_in[:, o_ua + k * d:o_ua + (k + 1) * d].astype(BF16) for k in range(4))

    cosf, sinf = _rope_tables(positions)
    q, kc, vc, kst, vs, kwt, vw, gates, z, ga, gc = _proj(
        x, row_vec(norm_mix_g), cosf, sinf, wq, wkv, wgn, wua, wub, wga, wgc, tm=PROJ_ROWS)

    n_chunks = s // CMP_STRIDE
    chunk_w = CMP_STRIDE * HEAD_DIM
    kct = _compress(kc.reshape(b, N_KV, n_chunks, chunk_w), cmp_k_pos.reshape(1, -1),
                    cmp_k_w1.astype(BF16), cmp_k_w2.astype(BF16), True)
    vcp = _compress(vc.reshape(b, N_KV, n_chunks, chunk_w), cmp_v_pos.reshape(1, -1),
                    cmp_v_w1.astype(BF16), cmp_v_w2.astype(BF16), False)

    overlap, nege, cmask, band, tri = _mask_tables(s)
    vc_ov = jnp.concatenate([vcp, jnp.broadcast_to(overlap, (b, N_KV) + overlap.shape)], axis=-1)
    attn = _nsa(q, kct, vc_ov, kst, vs, kwt, vw, gates, nege, cmask, band, tri)

    x1 = _mix(x, z, attn, ga, gc, conv_w.reshape(CONV_WIDTH, d), row_vec(conv_b), row_vec(conv_norm_g),
              row_vec(conv_norm_b), w_conv_out.astype(BF16), row_vec(b_conv_out), w_out.astype(BF16), tm=MIX_ROWS)

    out = _ffn(x1.reshape(b * s, d), row_vec(norm_ffn_g), w_ffn_gate.astype(BF16), w_ffn_up.astype(BF16),
               w_ffn_down.astype(BF16), row_vec(norm_final_g), tm=FFN_ROWS, chunk=FFN_CHUNK)
    return out.reshape(b, s, d)
```
